```python
import math
import jax
import jax.numpy as jnp
from jax import lax
import numpy as np

D_MODEL = 2048
BATCH = 8
SEQ = 4096
DEPTH = 2

GRID_W = 64
CTX_LEN = 256
N_EVEN = (DEPTH + 1) // 2
N_ODD = DEPTH // 2
N_MOD = 6
NORM_EPS = 1e-6

HGRN_WIDTH = D_MODEL // 2
HGRN_HEAD_DIM = 128
HGRN_HEADS = HGRN_WIDTH // HGRN_HEAD_DIM
HGRN_CHUNK = 64

DATTN_WIDTH = D_MODEL // 2
DATTN_HEAD_DIM = 64
DATTN_HEADS = DATTN_WIDTH // (2 * DATTN_HEAD_DIM)
Q_BLOCK = 128
ROPE_BASE = 10000.0

MIX_IN = 5 * HGRN_WIDTH + 3 * DATTN_WIDTH
MIX_SPLITS = [HGRN_WIDTH * i for i in range(1, 6)] + [5 * HGRN_WIDTH + DATTN_WIDTH * i for i in range(1, 3)]
MIX_OUT = HGRN_WIDTH + DATTN_WIDTH

CONV_WIDTH = D_MODEL
CONV_KERNEL = 31

N_EXPERTS = 32
TOP_K = 4
EXPERT_FF = D_MODEL
SWIGLU_ALPHA = 1.702
SWIGLU_LIMIT = 7.0
MOE_BLOCK = 128

kernel_name = 'hybrid_hgrn2_diffattn_conformer_moe_dit'

F32 = jnp.float32


def _rmsnorm(x, g):
    xf = x.astype(F32)
    y = xf * lax.rsqrt(jnp.mean(jnp.square(xf), axis=-1, keepdims=True) + NORM_EPS)
    return (y * g.astype(F32)).astype(x.dtype)


def _layernorm(x, g, b):
    xf = x.astype(F32)
    mu = jnp.mean(xf, axis=-1, keepdims=True)
    xc = xf - mu
    y = xc * lax.rsqrt(jnp.mean(jnp.square(xc), axis=-1, keepdims=True) + NORM_EPS)
    return (y * g.astype(F32) + b.astype(F32)).astype(x.dtype)


def _modulate(h, shift, scale):
    return h * (1 + scale) + shift


def _heads(a, n_heads):
    b, t, _ = a.shape
    return a.reshape(b, t, n_heads, -1).transpose(0, 2, 1, 3)


def _merge_heads(a):
    b, h, t, d = a.shape
    return a.transpose(0, 2, 1, 3).reshape(b, t, h * d)


def _axial_rope(rows, dim):
    row = jnp.repeat(jnp.arange(rows, dtype=F32), GRID_W)
    col = jnp.tile(jnp.arange(GRID_W, dtype=F32), rows)
    n_freq = dim // 4
    inv = ROPE_BASE ** (-jnp.arange(n_freq, dtype=F32) / n_freq)
    ang = jnp.stack([row[:, None] * inv, col[:, None] * inv], axis=1)
    return jnp.cos(ang), jnp.sin(ang)


def _apply_axial_rope(x, cos, sin):
    xs = x.reshape(x.shape[:-1] + (2, 2, x.shape[-1] // 4))
    x1, x2 = xs[..., 0, :], xs[..., 1, :]
    cos = cos.astype(x.dtype)
    sin = sin.astype(x.dtype)
    out = jnp.stack([x1 * cos - x2 * sin, x2 * cos + x1 * sin], axis=-2)
    return out.reshape(x.shape)


def _hgrn_gate(z, lb):
    zf = z.astype(F32)
    logf = jnp.log(lb + (1.0 - lb) * jax.nn.sigmoid(zf))
    k = (1.0 - lb) * jax.nn.sigmoid(-zf)
    return logf, k.astype(z.dtype)


def _hgrn_chunk_scan(q, k, v, logf, s0):
    b, h, t, _ = q.shape
    n_chunks = t // HGRN_CHUNK

    def chunks(a):
        return jnp.moveaxis(a.reshape(b, h, n_chunks, HGRN_CHUNK, a.shape[-1]), 2, 0)

    lower = jnp.tril(jnp.ones((HGRN_CHUNK, HGRN_CHUNK), dtype=bool))[:, :, None]

    def step(s, inp):
        qc, kc, vc, lc = inp
        qf, kf, vf = qc.astype(F32), kc.astype(F32), vc.astype(F32)
        cum = jnp.cumsum(lc, axis=2)
        rel = jnp.where(lower, cum[:, :, :, None, :] - cum[:, :, None, :, :], -jnp.inf)
        att = jnp.einsum('bhtk,bhsk,bhtsk->bhts', qf, kf, jnp.exp(rel))
        o = jnp.einsum('bhts,bhsv->bhtv', att, vf) + jnp.einsum('bhtk,bhkv->bhtv', qf * jnp.exp(cum), s)
        tot = cum[:, :, -1:, :]
        s_new = jnp.exp(tot[:, :, 0, :, None]) * s + jnp.einsum('bhsk,bhsv->bhkv', kf * jnp.exp(tot - cum), vf)
        return s_new, o

    s_fin, o = lax.scan(step, s0, (chunks(q), chunks(k), chunks(v), chunks(logf)))
    return jnp.moveaxis(o, 0, 2).reshape(b, h, t, -1).astype(q.dtype), s_fin


def _hgrn_final_state(k, v, logf):
    cum = jnp.cumsum(logf, axis=2)
    w = k.astype(F32) * jnp.exp(cum[:, :, -1:, :] - cum)
    return jnp.einsum('bhsk,bhsv->bhkv', w, v.astype(F32))


def _hgrn_direction(q, z, v, zc, vc, lb, reverse, qc):
    if reverse:
        q, z, v, zc, vc = [jnp.flip(a, axis=2) for a in (q, z, v, zc, vc)]
        qc = None if qc is None else jnp.flip(qc, axis=2)
    logf, k = _hgrn_gate(z, lb)
    logfc, kc = _hgrn_gate(zc, lb)
    if qc is None:
        s_ctx = _hgrn_final_state(kc, vc, logfc)
        oc = None
    else:
        b, h, _, dk = kc.shape
        s0 = jnp.zeros((b, h, dk, vc.shape[-1]), F32)
        oc, s_ctx = _hgrn_chunk_scan(qc, kc, vc, logfc, s0)
    o, _ = _hgrn_chunk_scan(q, k, v, logf, s_ctx)
    if reverse:
        o = jnp.flip(o, axis=2)
        oc = None if oc is None else jnp.flip(oc, axis=2)
    return o, oc


def _dattn_qk_heads(a):
    b, t, _ = a.shape
    return a.reshape(b, t, DATTN_HEADS, 2, DATTN_HEAD_DIM).transpose(0, 2, 3, 1, 4)


def _diff_attn_scores(q, k, v, lam):
    s = jnp.einsum('bhmqd,bhmkd->bhmqk', q, k).astype(F32) * (DATTN_HEAD_DIM ** -0.5)
    p = jax.nn.softmax(s, axis=-1)
    pd = p[:, :, 0] - lam * p[:, :, 1]
    return jnp.einsum('bhqk,bhkv->bhqv', pd.astype(v.dtype), v)


def _diff_attention_latent(q, k, v, kc, vc, lam):
    kk = jnp.concatenate([kc, k], axis=3)
    vv = jnp.concatenate([vc, v], axis=2)
    b, h, m, t, d = q.shape
    n_blk = t // Q_BLOCK
    qb = jnp.moveaxis(q.reshape(b, h, m, n_blk, Q_BLOCK, d), 3, 0)
    o = lax.map(lambda blk: _diff_attn_scores(blk, kk, vv, lam), qb)
    return jnp.moveaxis(o, 0, 2).reshape(b, h, t, -1)


def _even_mixer(h, hc, w_in, w_out, lb_logits, hgrn_norm_g, dattn_lambda, subln_g, j, layer, cos, sin, need_ctx_out):
    hq, hi, hff, hfb, hg, aq, ak, av = jnp.split(h @ w_in, MIX_SPLITS, axis=-1)
    chq, chi, chff, chfb, chg, caq, cak, cav = jnp.split(hc @ w_in, MIX_SPLITS, axis=-1)

    p_lb = jax.nn.softmax(lb_logits.astype(F32), axis=1)
    lb = jnp.cumsum(p_lb, axis=1)[:, j].reshape(2, HGRN_HEADS, 1, HGRN_HEAD_DIM)
    hd = lambda a: _heads(a, HGRN_HEADS)
    q_h, v_h, cv_h = hd(hq), hd(hi), hd(chi)
    cq_h = hd(chq) if need_ctx_out else None
    o_f, oc_f = _hgrn_direction(q_h, hd(hff), v_h, hd(chff), cv_h, lb[0], False, cq_h)
    o_b, oc_b = _hgrn_direction(q_h, hd(hfb), v_h, hd(chfb), cv_h, lb[1], True, cq_h)

    lam_init = 0.8 - 0.6 * math.exp(-0.3 * layer)
    lp = dattn_lambda.astype(F32)
    lam = jnp.exp(jnp.sum(lp[0] * lp[1])) - jnp.exp(jnp.sum(lp[2] * lp[3])) + lam_init
    q_a = _apply_axial_rope(_dattn_qk_heads(aq), cos, sin)
    k_a = _apply_axial_rope(_dattn_qk_heads(ak), cos, sin)
    kc_a = _dattn_qk_heads(cak)
    vc_a = _heads(cav, DATTN_HEADS)
    o_a = _diff_attention_latent(q_a, k_a, _heads(av, DATTN_HEADS), kc_a, vc_a, lam)

    def merge(o_h, g_h, o_at):
        y_h = _merge_heads(_rmsnorm(o_h, hgrn_norm_g) * jax.nn.silu(g_h))
        y_a = _merge_heads(_rmsnorm(o_at, subln_g) * (1.0 - lam_init))
        return jnp.concatenate([y_h, y_a], axis=-1) @ w_out

    y = merge(o_f + o_b, hd(hg), o_a)
    yc = None
    if need_ctx_out:
        oc_a = _diff_attn_scores(_dattn_qk_heads(caq), kc_a, vc_a, lam)
        yc = merge(oc_f + oc_b, hd(chg), oc_a)
    return y, yc


def _conformer_conv(h, w1, b1, w_dw, b_dw, ln_g, ln_b, w2, b2):
    a = h @ w1 + b1
    a = a[..., :CONV_WIDTH] * jax.nn.sigmoid(a[..., CONV_WIDTH:])
    a = lax.conv_general_dilated(a, w_dw[:, None, :].astype(a.dtype), window_strides=(1,), padding='SAME',
                                 dimension_numbers=('NWC', 'WIO', 'NWC'),
                                 feature_group_count=CONV_WIDTH) + b_dw
    a = jax.nn.silu(_layernorm(a, ln_g, ln_b))
    return a @ w2 + b2


def _clamped_swiglu(gu):
    g, u = gu[..., :EXPERT_FF], gu[..., EXPERT_FF:]
    g = jnp.minimum(g, SWIGLU_LIMIT)
    u = jnp.clip(u, -SWIGLU_LIMIT, SWIGLU_LIMIT)
    return g * jax.nn.sigmoid(SWIGLU_ALPHA * g) * (u + 1)


def _moe(h, w_router, b_router, w_gu, b_gu, w_down, b_down):
    n, d = h.shape
    logits = (h @ w_router + b_router).astype(F32)
    top_val, top_idx = lax.top_k(logits, TOP_K)
    gate = jax.nn.softmax(top_val, axis=-1).astype(h.dtype)
    flat_e = top_idx.reshape(-1)
    flat_tok = jnp.repeat(jnp.arange(n, dtype=jnp.int32), TOP_K)
    flat_g = gate.reshape(-1)
    order = jnp.argsort(flat_e)
    se = flat_e[order]
    counts = jnp.zeros((N_EXPERTS,), jnp.int32).at[flat_e].add(1)
    padded = (counts + MOE_BLOCK - 1) // MOE_BLOCK * MOE_BLOCK
    pend = jnp.cumsum(padded)
    pstart = pend - padded
    start = jnp.cumsum(counts) - counts
    dest = pstart[se] + jnp.arange(n * TOP_K, dtype=jnp.int32) - start[se]
    n_blocks = -(-(n * TOP_K) // MOE_BLOCK) + N_EXPERTS
    n_slots = n_blocks * MOE_BLOCK
    slot_tok = jnp.full((n_slots,), n, jnp.int32).at[dest].set(flat_tok[order])
    slot_g = jnp.zeros((n_slots,), h.dtype).at[dest].set(flat_g[order])
    block_e = jnp.minimum(jnp.searchsorted(pend, jnp.arange(n_blocks, dtype=jnp.int32) * MOE_BLOCK, side='right'),
                          N_EXPERTS - 1)
    h_pad = jnp.concatenate([h, jnp.zeros((1, d), h.dtype)], axis=0)
    xb = h_pad[slot_tok].reshape(n_blocks, MOE_BLOCK, d)

    def expert_block(args):
        xblk, e = args
        return _clamped_swiglu(xblk @ w_gu[e] + b_gu[e]) @ w_down[e] + b_down[e]

    yb = lax.map(expert_block, (xb, block_e))
    y = yb.reshape(n_slots, d) * slot_g[:, None]
    return jnp.zeros_like(h_pad).at[slot_tok].add(y)[:n]


def _moe_tokens(h, w_router, b_router, w_gu, b_gu, w_down, b_down):
    b, t, d = h.shape
    return _moe(h.reshape(b * t, d), w_router, b_router, w_gu, b_gu, w_down, b_down).reshape(b, t, d)


def setup_inputs(seed: int = 0) -> dict:
    key = jax.random.key(seed)
    ks = iter(jax.random.split(key, 32))
    nrm = lambda shape, scale: jax.random.normal(next(ks), shape, jnp.float32) * scale
    gain = lambda shape: 1.0 + nrm(shape, 0.02)
    return {
        'x': nrm((BATCH, SEQ, D_MODEL), 1.0),
        'c': nrm((BATCH, D_MODEL), 1.0),
        'ctx': nrm((BATCH, CTX_LEN, D_MODEL), 1.0),
        'c_ctx': nrm((D_MODEL,), 1.0),
        'norm_mix_g': gain((DEPTH, D_MODEL)),
        'norm_ffn_g': gain((DEPTH, D_MODEL)),
        'ada_w': nrm((DEPTH, D_MODEL, N_MOD * D_MODEL), 0.5 * D_MODEL ** -0.5),
        'ada_b': nrm((DEPTH, N_MOD * D_MODEL), 0.02),
        'ev_w_in': nrm((N_EVEN, D_MODEL, MIX_IN), D_MODEL ** -0.5),
        'ev_w_out': nrm((N_EVEN, MIX_OUT, D_MODEL), MIX_OUT ** -0.5),
        'ev_hgrn_lb_logits': nrm((2, N_EVEN + 1, HGRN_WIDTH), 0.5),
        'ev_hgrn_norm_g': gain((N_EVEN, HGRN_HEAD_DIM)),
        'ev_dattn_lambda': nrm((N_EVEN, 4, DATTN_HEAD_DIM), 0.1),
        'ev_dattn_subln_g': gain((N_EVEN, 2 * DATTN_HEAD_DIM)),
        'od_w_pw1': nrm((N_ODD, D_MODEL, 2 * CONV_WIDTH), D_MODEL ** -0.5),
        'od_b_pw1': nrm((N_ODD, 2 * CONV_WIDTH), 0.01),
        'od_w_dw': nrm((N_ODD, CONV_KERNEL, CONV_WIDTH), CONV_KERNEL ** -0.5),
        'od_b_dw': nrm((N_ODD, CONV_WIDTH), 0.01),
        'od_ln_g': gain((N_ODD, CONV_WIDTH)),
        'od_ln_b': nrm((N_ODD, CONV_WIDTH), 0.01),
        'od_w_pw2': nrm((N_ODD, CONV_WIDTH, D_MODEL), CONV_WIDTH ** -0.5),
        'od_b_pw2': nrm((N_ODD, D_MODEL), 0.01),
        'moe_w_router': nrm((DEPTH, D_MODEL, N_EXPERTS), D_MODEL ** -0.5),
        'moe_b_router': nrm((DEPTH, N_EXPERTS), 0.01),
        'moe_w_gu': nrm((DEPTH, N_EXPERTS, D_MODEL, 2 * EXPERT_FF), D_MODEL ** -0.5),
        'moe_b_gu': nrm((DEPTH, N_EXPERTS, 2 * EXPERT_FF), 0.01),
        'moe_w_down': nrm((DEPTH, N_EXPERTS, EXPERT_FF, D_MODEL), EXPERT_FF ** -0.5),
        'moe_b_down': nrm((DEPTH, N_EXPERTS, D_MODEL), 0.01),
        'final_norm_g': gain((D_MODEL,)),
    }


def reference(x, c, ctx, c_ctx, norm_mix_g, norm_ffn_g, ada_w, ada_b,
              ev_w_in, ev_w_out, ev_hgrn_lb_logits, ev_hgrn_norm_g, ev_dattn_lambda, ev_dattn_subln_g,
              od_w_pw1, od_b_pw1, od_w_dw, od_b_dw, od_ln_g, od_ln_b, od_w_pw2, od_b_pw2,
              moe_w_router, moe_b_router, moe_w_gu, moe_b_gu, moe_w_down, moe_b_down, final_norm_g):
    rows = x.shape[1] // GRID_W
    cos, sin = _axial_rope(rows, DATTN_HEAD_DIM)
    silu_c = jax.nn.silu(c)
    silu_cc = jax.nn.silu(c_ctx)
    for l in range(DEPTH):
        j = l // 2
        need_next = any(i % 2 == 0 for i in range(l + 1, DEPTH))
        ctx_used = (l % 2 == 0) or need_next
        sh1, sc1, g1, sh2, sc2, g2 = jnp.split((silu_c @ ada_w[l] + ada_b[l])[:, None, :], N_MOD, axis=-1)
        h = _modulate(_rmsnorm(x, norm_mix_g[l]), sh1, sc1)
        if ctx_used:
            csh1, csc1, cg1, csh2, csc2, cg2 = jnp.split(silu_cc @ ada_w[l] + ada_b[l], N_MOD, axis=-1)
            hc = _modulate(_rmsnorm(ctx, norm_mix_g[l]), csh1, csc1)
        if l % 2 == 0:
            y, yc = _even_mixer(h, hc, ev_w_in[j], ev_w_out[j], ev_hgrn_lb_logits, ev_hgrn_norm_g[j],
                                ev_dattn_lambda[j], ev_dattn_subln_g[j], j, l, cos, sin, need_next)
        else:
            conv = lambda a: _conformer_conv(a, od_w_pw1[j], od_b_pw1[j], od_w_dw[j], od_b_dw[j],
                                             od_ln_g[j], od_ln_b[j], od_w_pw2[j], od_b_pw2[j])
            y = conv(h)
            yc = conv(hc) if need_next else None
        x = x + g1 * y
        moe_p = (moe_w_router[l], moe_b_router[l], moe_w_gu[l], moe_b_gu[l], moe_w_down[l], moe_b_down[l])
        x = x + g2 * _moe_tokens(_modulate(_rmsnorm(x, norm_ffn_g[l]), sh2, sc2), *moe_p)
        if need_next:
            ctx = ctx + cg1 * yc
            ctx = ctx + cg2 * _moe_tokens(_modulate(_rmsnorm(ctx, norm_ffn_g[l]), csh2, csc2), *moe_p)
    return _rmsnorm(x, final_norm_g)
```

```python
import functools
import math

import numpy as np
import jax
import jax.numpy as jnp
from jax import lax
from jax.experimental import pallas as pl
from jax.experimental.pallas import tpu as pltpu

F32 = jnp.float32
BF16 = jnp.bfloat16

NORM_EPS = 1e-6
GRID_W = 64
N_MOD = 6
HEAD = 128
DATTN_HEAD_DIM = 64
ROPE_BASE = 10000.0
CONV_KERNEL = 31
CONV_HALO = 16
N_EXPERTS = 32
TOP_K = 4
SWIGLU_ALPHA = 1.702
SWIGLU_LIMIT = 7.0
LANES = 128
HGRN_CHUNK = 128
NEG_BIG = -1e30

VMEM_LIMIT = 56 * 1024 * 1024


def _cparams(sem):
    return pltpu.CompilerParams(dimension_semantics=sem, vmem_limit_bytes=VMEM_LIMIT)


def _dot(a, b):
    return jnp.dot(a, b, preferred_element_type=F32)


def _dot_nt(a, b):
    return lax.dot_general(a, b, (((1,), (1,)), ((), ())), preferred_element_type=F32)


def _dot_tn(a, b):
    return lax.dot_general(a, b, (((0,), (0,)), ((), ())), preferred_element_type=F32)


def _split3(x):
    hi = x.astype(BF16)
    r1 = x - hi.astype(F32)
    mid = r1.astype(BF16)
    lo = (r1 - mid.astype(F32)).astype(BF16)
    return hi, mid, lo


def _norm_mod(x, g, shift, scale):
    ms = jnp.mean(x * x, axis=-1, keepdims=True)
    return (x * lax.rsqrt(ms + NORM_EPS) * g) * (1.0 + scale) + shift


def _ada_kernel(cs_ref, w_ref, b_ref, o_ref):
    cs = cs_ref[...]
    s = cs * jax.nn.sigmoid(cs)
    hi, mid, lo = _split3(s)
    wh, wm, wl = _split3(w_ref[...])
    acc = _dot(hi, wh) + (_dot(hi, wm) + _dot(mid, wh)) + (_dot(hi, wl) + _dot(mid, wm) + _dot(lo, wh))
    o_ref[...] = acc + b_ref[...]


def _ada_call(cs, ada_w, ada_b):
    n_layers, d, n6 = ada_w.shape
    r = cs.shape[0]
    tn = min(512, n6)
    return pl.pallas_call(
        _ada_kernel,
        grid=(n_layers, n6 // tn),
        in_specs=[
            pl.BlockSpec((r, d), lambda l, j: (0, 0)),
            pl.BlockSpec((None, d, tn), lambda l, j: (l, 0, j)),
            pl.BlockSpec((None, 1, tn), lambda l, j: (l, 0, j)),
        ],
        out_specs=pl.BlockSpec((None, r, tn), lambda l, j: (l, 0, j)),
        out_shape=jax.ShapeDtypeStruct((n_layers, r, n6), F32),
        compiler_params=_cparams(("parallel", "parallel")),
        name="ada_mod",
    )(cs, ada_w, ada_b.reshape(n_layers, 1, n6))


def _proj_kernel(x_ref, g_ref, sh_ref, sc_ref, w_ref, o_ref, h_ref):
    @pl.when(pl.program_id(1) == 0)
    def _():
        h_ref[...] = _norm_mod(x_ref[...], g_ref[...], sh_ref[...], sc_ref[...]).astype(BF16)

    o_ref[...] = _dot(h_ref[...], w_ref[...]).astype(o_ref.dtype)


def _proj_glu_kernel(x_ref, g_ref, sh_ref, sc_ref, wa_ref, wb_ref, ba_ref, bb_ref, o_ref, h_ref):
    @pl.when(pl.program_id(1) == 0)
    def _():
        h_ref[...] = _norm_mod(x_ref[...], g_ref[...], sh_ref[...], sc_ref[...]).astype(BF16)

    h = h_ref[...]
    a = _dot(h, wa_ref[...]) + ba_ref[...]
    b = _dot(h, wb_ref[...]) + bb_ref[...]
    o_ref[...] = (a * jax.nn.sigmoid(b)).astype(o_ref.dtype)


def _proj_call(x2, g, shift, scale, rows_per_mod, w, glu_bias=None, tm=1024, tn=1024):
    m, d = x2.shape
    n_out = w.shape[1] if glu_bias is None else w.shape[1] // 2
    tm = min(tm, m, rows_per_mod)
    tn = min(tn, n_out)
    bpm = rows_per_mod // tm
    row_spec = pl.BlockSpec((tm, d), lambda i, j: (i, 0))
    vec_spec = pl.BlockSpec((1, d), lambda i, j: (0, 0))
    mod_spec = pl.BlockSpec((None, 1, d), lambda i, j: (i // bpm, 0, 0))
    out_spec = pl.BlockSpec((tm, tn), lambda i, j: (i, j))
    common = dict(
        grid=(m // tm, n_out // tn),
        out_specs=out_spec,
        out_shape=jax.ShapeDtypeStruct((m, n_out), BF16),
        scratch_shapes=[pltpu.VMEM((tm, d), BF16)],
        compiler_params=_cparams(("parallel", "arbitrary")),
    )
    if glu_bias is None:
        return pl.pallas_call(
            _proj_kernel,
            in_specs=[row_spec, vec_spec, mod_spec, mod_spec,
                      pl.BlockSpec((d, tn), lambda i, j: (0, j))],
            name="norm_proj",
            **common,
        )(x2, g, shift, scale, w)
    nb = n_out // tn
    bias = glu_bias.reshape(1, 2 * n_out)
    return pl.pallas_call(
        _proj_glu_kernel,
        in_specs=[row_spec, vec_spec, mod_spec, mod_spec,
                  pl.BlockSpec((d, tn), lambda i, j: (0, j)),
                  pl.BlockSpec((d, tn), lambda i, j: (0, j + nb)),
                  pl.BlockSpec((1, tn), lambda i, j: (0, j)),
                  pl.BlockSpec((1, tn), lambda i, j: (0, j + nb))],
        name="norm_proj_glu",
        **common,
    )(x2, g, shift, scale, w, w, bias, bias)


def _hgrn_tables(c):
    n_lvl = int(math.log2(c))
    t = np.arange(c)[:, None]
    s = np.arange(c)[None, :]
    gs = [(s <= t).astype(np.float32)]
    masks = [(s == t).astype(np.float32)]
    for lvl in range(1, n_lvl + 1):
        b = 2 ** (lvl - 1)
        mid = (t // (2 * b)) * (2 * b) + b - 1
        upper = t > mid
        g = np.where(upper & (s > mid) & (s <= t), 1.0, 0.0) - np.where((~upper) & (s > t) & (s <= mid), 1.0, 0.0)
        same = (t // (2 * b)) == (s // (2 * b))
        gs.append(g.astype(np.float32))
        masks.append((same & upper & (s <= mid)).astype(np.float32))
    gs_f = np.concatenate(gs, axis=0)
    mk_f = np.stack(masks, axis=0)
    gs_b = np.concatenate([g[::-1, ::-1] for g in gs], axis=0)
    mk_b = np.stack([m[::-1, ::-1] for m in masks], axis=0)
    return np.stack([gs_f, gs_b]), np.stack([mk_f, mk_b])


def _hgrn_gate(z, lb):
    e = jnp.exp(-jnp.abs(z))
    r = 1.0 / (1.0 + e)
    er = e * r
    pos = z >= 0
    sig = jnp.where(pos, r, er)
    sig_neg = jnp.where(pos, er, r)
    logf = jnp.log(lb + (1.0 - lb) * sig)
    return logf, (1.0 - lb) * sig_neg


def _gs_dot(gs, logf):
    hi, mid, lo = _split3(logf)
    return _dot(gs, hi) + _dot(gs, mid) + _dot(gs, lo)


def _hgrn_state_step(z, v, lb, tri, last_row, st):
    logf, k = _hgrn_gate(z, lb)
    cum = _gs_dot(tri, logf)
    tot = cum[last_row:last_row + 1, :]
    kt = (k * jnp.exp(tot - cum)).astype(BF16)
    return st * jnp.exp(tot) + _dot_tn(v, kt)


def _hgrn_chunk_step(q, z, v, lb, gs_ref, mk_ref, d, last_row, st):
    c = q.shape[0]
    n_lvl = mk_ref.shape[1] - 1
    logf, k = _hgrn_gate(z, lb)
    dist = _gs_dot(gs_ref[d], logf)
    cum = dist[0:c]
    tot = cum[last_row:last_row + 1, :]
    qf = q.astype(F32)
    att = mk_ref[d, 0] * _dot_nt(q, k.astype(BF16))
    for lvl in range(1, n_lvl + 1):
        e = jnp.exp(-jnp.abs(dist[lvl * c:(lvl + 1) * c]))
        att = att + mk_ref[d, lvl] * _dot_nt((qf * e).astype(BF16), (k * e).astype(BF16))
    o = _dot(att.astype(BF16), v) + _dot_nt((qf * jnp.exp(cum)).astype(BF16), st.astype(BF16))
    kt = (k * jnp.exp(tot - cum)).astype(BF16)
    st_new = st * jnp.exp(tot) + _dot_tn(v, kt)
    return o, st_new


def _hgrn_kernel(q_ref, v_ref, zf_ref, zb_ref, g_ref, vc_ref, zfc_ref, zbc_ref, lb_ref, gn_ref,
                 gs_ref, mk_ref, o_ref, acc_ref):
    c = HGRN_CHUNK
    t_len = q_ref.shape[0]
    n = t_len // c
    nc = vc_ref.shape[0] // c
    lbf = lb_ref[0]
    lbb = lb_ref[1]
    acc_ref[...] = jnp.zeros_like(acc_ref)
    st0 = jnp.zeros((HEAD, HEAD), F32)

    def rows(i):
        return pl.ds(pl.multiple_of(i * c, c), c)

    def ctx_body(i, carry):
        sf, sb = carry
        rf = rows(i)
        rb = rows(nc - 1 - i)
        sf = _hgrn_state_step(zfc_ref[rf, :].astype(F32), vc_ref[rf, :], lbf, gs_ref[0, 0:c, :], c - 1, sf)
        sb = _hgrn_state_step(zbc_ref[rb, :].astype(F32), vc_ref[rb, :], lbb, gs_ref[1, 0:c, :], 0, sb)
        return sf, sb

    sf, sb = lax.fori_loop(0, nc, ctx_body, (st0, st0))

    def body(i, carry):
        sf, sb = carry
        rf = rows(i)
        rb = rows(n - 1 - i)
        of, sf = _hgrn_chunk_step(q_ref[rf, :], zf_ref[rf, :].astype(F32), v_ref[rf, :], lbf,
                                  gs_ref, mk_ref, 0, c - 1, sf)
        acc_ref[rf, :] += of
        ob, sb = _hgrn_chunk_step(q_ref[rb, :], zb_ref[rb, :].astype(F32), v_ref[rb, :], lbb,
                                  gs_ref, mk_ref, 1, 0, sb)
        acc_ref[rb, :] += ob
        return sf, sb

    lax.fori_loop(0, n, body, (sf, sb))

    o = acc_ref[...]
    gate = g_ref[...].astype(F32)
    y = o * lax.rsqrt(jnp.mean(o * o, axis=-1, keepdims=True) + NORM_EPS) * gn_ref[...]
    o_ref[...] = (y * (gate * jax.nn.sigmoid(gate))).astype(o_ref.dtype)


def _hgrn_call(p3, pc3, lb, gn, n_heads):
    b, t, _ = p3.shape
    cl = pc3.shape[1]
    h = n_heads
    gs, mk = _hgrn_tables(HGRN_CHUNK)
    gs = jnp.asarray(gs, BF16)
    mk = jnp.asarray(mk, F32)

    def col(seg, rows_):
        return pl.BlockSpec((None, rows_, HEAD), lambda bi, hi, seg=seg: (bi, 0, seg * h + hi))

    return pl.pallas_call(
        _hgrn_kernel,
        grid=(b, h),
        in_specs=[col(0, t), col(1, t), col(2, t), col(3, t), col(4, t),
                  col(1, cl), col(2, cl), col(3, cl),
                  pl.BlockSpec((2, None, 1, HEAD), lambda bi, hi: (0, hi, 0, 0)),
                  pl.BlockSpec((1, HEAD), lambda bi, hi: (0, 0)),
                  pl.BlockSpec(gs.shape, lambda bi, hi: (0, 0, 0)),
                  pl.BlockSpec(mk.shape, lambda bi, hi: (0, 0, 0, 0))],
        out_specs=pl.BlockSpec((None, t, HEAD), lambda bi, hi: (bi, 0, hi)),
        out_shape=jax.ShapeDtypeStruct((b, t, h * HEAD), BF16),
        scratch_shapes=[pltpu.VMEM((t, HEAD), F32)],
        compiler_params=_cparams(("parallel", "parallel")),
        name="hgrn2",
    )(p3, p3, p3, p3, p3, pc3, pc3, pc3, lb, gn, gs, mk)


def _rope_tables(t_len):
    rows = t_len // GRID_W
    row = jnp.repeat(jnp.arange(rows, dtype=F32), GRID_W)
    colp = jnp.tile(jnp.arange(GRID_W, dtype=F32), rows)
    n_freq = DATTN_HEAD_DIM // 4
    inv = ROPE_BASE ** (-jnp.arange(n_freq, dtype=F32) / n_freq)
    ang_r = row[:, None] * inv
    ang_c = colp[:, None] * inv
    cos64 = jnp.concatenate([jnp.cos(ang_r), jnp.cos(ang_r), jnp.cos(ang_c), jnp.cos(ang_c)], axis=1)
    sin64 = jnp.concatenate([-jnp.sin(ang_r), jnp.sin(ang_r), -jnp.sin(ang_c), jnp.sin(ang_c)], axis=1)
    lane = np.arange(HEAD)
    partner = np.where(lane % (2 * n_freq) < n_freq, lane + n_freq, lane - n_freq)
    perm = np.zeros((HEAD, HEAD), np.float32)
    perm[partner, lane] = 1.0
    return jnp.tile(cos64, (1, 2)), jnp.tile(sin64, (1, 2)), jnp.asarray(perm, BF16)


def _attn_kernel(lam_ref, q_ref, k_ref, v_ref, kc_ref, vc_ref, cos_ref, sin_ref, perm_ref, sg_ref,
                 o_ref, kall_ref, vall_ref, *, out_scale):
    qi = pl.program_id(2)
    tq = q_ref.shape[0]
    cl = kc_ref.shape[0]

    @pl.when(qi == 0)
    def _():
        k = k_ref[...]
        kr = k.astype(F32) * cos_ref[...] + _dot(k, perm_ref[...]) * sin_ref[...]
        kall_ref[cl:, :] = kr.astype(BF16)
        kall_ref[0:cl, :] = kc_ref[...]
        vall_ref[cl:, :] = v_ref[...]
        vall_ref[0:cl, :] = vc_ref[...]

    r0 = pl.multiple_of(qi * tq, tq)
    q = q_ref[...]
    cq = cos_ref[pl.ds(r0, tq), :]
    sq = sin_ref[pl.ds(r0, tq), :]
    qr = (q.astype(F32) * cq + _dot(q, perm_ref[...]) * sq) * (DATTN_HEAD_DIM ** -0.5)
    lane = lax.broadcasted_iota(jnp.int32, (tq, HEAD), 1)
    q0 = jnp.where(lane < DATTN_HEAD_DIM, qr, 0.0).astype(BF16)
    q1 = jnp.where(lane >= DATTN_HEAD_DIM, qr, 0.0).astype(BF16)
    lam = lam_ref[0]

    def softmax_rows(qm):
        s = _dot_nt(qm, kall_ref[...])
        m = jnp.max(s, axis=-1, keepdims=True)
        p = jnp.exp(s - m)
        return p, 1.0 / jnp.sum(p, axis=-1, keepdims=True)

    p0, i0 = softmax_rows(q0)
    p1, i1 = softmax_rows(q1)
    pd = p0 * i0 - p1 * (lam * i1)
    o = _dot(pd.astype(BF16), vall_ref[...])
    y = o * lax.rsqrt(jnp.mean(o * o, axis=-1, keepdims=True) + NORM_EPS) * sg_ref[...] * out_scale
    o_ref[...] = y.astype(o_ref.dtype)


def _attn_call(p3, pc3, lam, subln_g, n_hgrn_heads, n_heads, out_scale, tq=128):
    b, t, _ = p3.shape
    cl = pc3.shape[1]
    tq = min(tq, t)
    base = 5 * n_hgrn_heads
    cos_t, sin_t, perm = _rope_tables(t)

    def col(seg, rows_):
        return pl.BlockSpec((None, rows_, HEAD), lambda bi, hi, qi, seg=seg: (bi, 0, base + seg * n_heads + hi))

    full2 = lambda shape: pl.BlockSpec(shape, lambda bi, hi, qi: (0, 0))
    return pl.pallas_call(
        functools.partial(_attn_kernel, out_scale=out_scale),
        grid=(b, n_heads, t // tq),
        in_specs=[pl.BlockSpec(memory_space=pltpu.SMEM),
                  pl.BlockSpec((None, tq, HEAD), lambda bi, hi, qi: (bi, qi, base + hi)),
                  col(1, t), col(2, t), col(1, cl), col(2, cl),
                  full2((t, HEAD)), full2((t, HEAD)), full2((HEAD, HEAD)), full2((1, HEAD))],
        out_specs=pl.BlockSpec((None, tq, HEAD), lambda bi, hi, qi: (bi, qi, hi)),
        out_shape=jax.ShapeDtypeStruct((b, t, n_heads * HEAD), BF16),
        scratch_shapes=[pltpu.VMEM((cl + t, HEAD), BF16), pltpu.VMEM((cl + t, HEAD), BF16)],
        compiler_params=_cparams(("parallel", "parallel", "arbitrary")),
        name="diff_attn",
    )(lam, p3, p3, p3, pc3, pc3, cos_t, sin_t, perm, subln_g)


def _resid_router_kernel(*refs, n_act, has_bias):
    acts = refs[0:n_act]
    ws = refs[n_act:2 * n_act]
    pos = 2 * n_act
    bias_ref = refs[pos] if has_bias else None
    pos += int(has_bias)
    x_ref, g1_ref, gf_ref, sh_ref, sc_ref, wrh_ref, wrl_ref, br_ref = refs[pos:pos + 8]
    x1_ref, h2_ref, ti_ref, gt_ref = refs[pos + 8:pos + 12]

    y = _dot(acts[0][...], ws[0][...])
    for a, w in zip(acts[1:], ws[1:]):
        y = y + _dot(a[...], w[...])
    if has_bias:
        y = y + bias_ref[...]
    x1 = x_ref[...] + g1_ref[...] * y
    x1_ref[...] = x1
    h2 = _norm_mod(x1, gf_ref[...], sh_ref[...], sc_ref[...])
    hi = h2.astype(BF16)
    h2_ref[...] = hi
    lo = (h2 - hi.astype(F32)).astype(BF16)
    wrh = wrh_ref[...]
    logits = _dot(hi, wrh) + (_dot(lo, wrh) + _dot(hi, wrl_ref[...])) + br_ref[...]

    tm = logits.shape[0]
    lane = lax.broadcasted_iota(jnp.int32, (tm, LANES), 1).astype(F32)
    vals, idxs = [], []
    rem = logits
    for _ in range(TOP_K):
        m = jnp.max(rem, axis=-1, keepdims=True)
        idx = jnp.min(jnp.where(rem == m, lane, float(LANES)), axis=-1, keepdims=True)
        vals.append(m)
        idxs.append(idx)
        rem = jnp.where(lane == idx, -jnp.inf, rem)
    es = [jnp.exp(v - vals[0]) for v in vals]
    inv = 1.0 / (es[0] + es[1] + es[2] + es[3])
    ti = jnp.zeros((tm, LANES), F32)
    gt = jnp.zeros((tm, LANES), F32)
    for k in range(TOP_K):
        ti = jnp.where(lane == float(k), idxs[k], ti)
        gt = jnp.where(lane == float(k), es[k] * inv, gt)
    ti_ref[...] = ti.astype(jnp.int32)
    gt_ref[...] = gt


def _resid_router_call(acts, ws, bias, x2, g1, gf, sh2, sc2, rows_per_mod, w_router, b_router, tm=256):
    m, d = x2.shape
    tm = min(tm, m, rows_per_mod)
    bpm = rows_per_mod // tm
    n_act = len(acts)
    wr = jnp.zeros((d, LANES), F32).at[:, :N_EXPERTS].set(w_router)
    wr_hi = wr.astype(BF16)
    wr_lo = (wr - wr_hi.astype(F32)).astype(BF16)
    br = jnp.full((1, LANES), NEG_BIG, F32).at[0, :N_EXPERTS].set(b_router)
    row = lambda width: pl.BlockSpec((tm, width), lambda i: (i, 0))
    full = lambda shape: pl.BlockSpec(shape, lambda i: (0, 0))
    mod_spec = pl.BlockSpec((None, 1, d), lambda i: (i // bpm, 0, 0))
    in_specs = [row(a.shape[1]) for a in acts] + [full(w.shape) for w in ws]
    args = list(acts) + list(ws)
    if bias is not None:
        in_specs.append(full((1, d)))
        args.append(bias.reshape(1, d))
    in_specs += [row(d), mod_spec, full((1, d)), mod_spec, mod_spec,
                 full((d, LANES)), full((d, LANES)), full((1, LANES))]
    args += [x2, g1, gf, sh2, sc2, wr_hi, wr_lo, br]
    return pl.pallas_call(
        functools.partial(_resid_router_kernel, n_act=n_act, has_bias=bias is not None),
        grid=(m // tm,),
        in_specs=in_specs,
        out_specs=[row(d), row(d), row(LANES), row(LANES)],
        out_shape=[jax.ShapeDtypeStruct((m, d), F32), jax.ShapeDtypeStruct((m, d), BF16),
                   jax.ShapeDtypeStruct((m, LANES), jnp.int32), jax.ShapeDtypeStruct((m, LANES), F32)],
        compiler_params=_cparams(("parallel",)),
        name="resid_router",
    )(*args)


def _moe_kernel(be_ref, bx_ref, bv_ref, x_ref, wg_ref, wu_ref, bg_ref, bu_ref, wd_ref, bd_ref, o_ref, acc_ref):
    i = pl.program_id(0)
    f = pl.program_id(1)
    last = pl.num_programs(1) - 1
    valid = bv_ref[i] > 0

    @pl.when(valid)
    def _():
        x = x_ref[...]
        g = _dot(x, wg_ref[...]) + bg_ref[...]
        u = _dot(x, wu_ref[...]) + bu_ref[...]
        g = jnp.minimum(g, SWIGLU_LIMIT)
        u = jnp.clip(u, -SWIGLU_LIMIT, SWIGLU_LIMIT)
        a = g * jax.nn.sigmoid(SWIGLU_ALPHA * g) * (u + 1.0)
        contrib = _dot(a.astype(BF16), wd_ref[...])

        @pl.when(f == 0)
        def _():
            acc_ref[...] = contrib

        @pl.when(f > 0)
        def _():
            acc_ref[...] += contrib

        @pl.when(f == last)
        def _():
            o_ref[...] = (acc_ref[...] + bd_ref[...]).astype(o_ref.dtype)

    @pl.when(jnp.logical_and(jnp.logical_not(valid), f == last))
    def _():
        o_ref[...] = jnp.zeros_like(o_ref)


def _moe_call(xs, blk_e, blk_x, blk_v, w_gu, b_gu, w_down, b_down, bm, tf=512):
    n_slots, d = xs.shape
    ff = w_down.shape[1]
    tf = min(tf, ff)
    nf = ff // tf
    nb = n_slots // bm
    fsel = lambda f, bv, i: jnp.where(bv[i] > 0, f, nf - 1)
    grid_spec = pltpu.PrefetchScalarGridSpec(
        num_scalar_prefetch=3,
        grid=(nb, nf),
        in_specs=[
            pl.BlockSpec((bm, d), lambda i, f, be, bx, bv: (bx[i], 0)),
            pl.BlockSpec((None, d, tf), lambda i, f, be, bx, bv: (be[i], 0, fsel(f, bv, i))),
            pl.BlockSpec((None, d, tf), lambda i, f, be, bx, bv: (be[i], 0, fsel(f, bv, i) + nf)),
            pl.BlockSpec((None, 1, tf), lambda i, f, be, bx, bv: (be[i], 0, fsel(f, bv, i))),
            pl.BlockSpec((None, 1, tf), lambda i, f, be, bx, bv: (be[i], 0, fsel(f, bv, i) + nf)),
            pl.BlockSpec((None, tf, d), lambda i, f, be, bx, bv: (be[i], fsel(f, bv, i), 0)),
            pl.BlockSpec((None, 1, d), lambda i, f, be, bx, bv: (be[i], 0, 0)),
        ],
        out_specs=pl.BlockSpec((bm, d), lambda i, f, be, bx, bv: (i, 0)),
        scratch_shapes=[pltpu.VMEM((bm, d), F32)],
    )
    n_e = w_gu.shape[0]
    return pl.pallas_call(
        _moe_kernel,
        grid_spec=grid_spec,
        out_shape=jax.ShapeDtypeStruct((n_slots, d), BF16),
        compiler_params=_cparams(("arbitrary", "arbitrary")),
        name="moe_experts",
    )(blk_e, blk_x, blk_v, xs, w_gu, w_gu, b_gu.reshape(n_e, 1, 2 * ff), b_gu.reshape(n_e, 1, 2 * ff),
      w_down, b_down.reshape(n_e, 1, d))


def _combine_kernel(x_ref, yg_ref, gt_ref, g2_ref, fg_ref, o_ref, *, final):
    gt = gt_ref[...]
    acc = yg_ref[0].astype(F32) * gt[:, 0:1]
    for k in range(1, TOP_K):
        acc = acc + yg_ref[k].astype(F32) * gt[:, k:k + 1]
    x2 = x_ref[...] + g2_ref[...] * acc
    if final:
        x2 = x2 * lax.rsqrt(jnp.mean(x2 * x2, axis=-1, keepdims=True) + NORM_EPS) * fg_ref[...]
    o_ref[...] = x2


def _combine_call(x1, yg, gates, g2, rows_per_mod, final_g, final, tm=256):
    m, d = x1.shape
    tm = min(tm, m, rows_per_mod)
    bpm = rows_per_mod // tm
    return pl.pallas_call(
        functools.partial(_combine_kernel, final=final),
        grid=(m // tm,),
        in_specs=[pl.BlockSpec((tm, d), lambda i: (i, 0)),
                  pl.BlockSpec((TOP_K, tm, d), lambda i: (0, i, 0)),
                  pl.BlockSpec((tm, LANES), lambda i: (i, 0)),
                  pl.BlockSpec((None, 1, d), lambda i: (i // bpm, 0, 0)),
                  pl.BlockSpec((1, d), lambda i: (0, 0))],
        out_specs=pl.BlockSpec((tm, d), lambda i: (i, 0)),
        out_shape=jax.ShapeDtypeStruct((m, d), F32),
        compiler_params=_cparams(("parallel",)),
        name="moe_combine",
    )(x1, yg, gates, g2, final_g.reshape(1, d))


def _moe_layer(x1, h2, topi, gates, g2, rows_per_mod, w_gu, b_gu, w_down, b_down, final_g, final, bm):
    n, d = x1.shape
    flat_e = topi[:, :TOP_K].reshape(-1)
    n_asn = n * TOP_K
    order = jnp.argsort(flat_e)
    se = flat_e[order]
    counts = jnp.zeros((N_EXPERTS,), jnp.int32).at[flat_e].add(1)
    padded = (counts + bm - 1) // bm * bm
    pend = jnp.cumsum(padded)
    pstart = pend - padded
    start = jnp.cumsum(counts) - counts
    dest = pstart[se] + jnp.arange(n_asn, dtype=jnp.int32) - start[se]
    n_blocks = -(-n_asn // bm) + N_EXPERTS
    n_slots = n_blocks * bm
    slot_tok = jnp.zeros((n_slots,), jnp.int32).at[dest].set((order // TOP_K).astype(jnp.int32))
    inv_slot = jnp.zeros((n_asn,), jnp.int32).at[order].set(dest)
    blk = jnp.arange(n_blocks, dtype=jnp.int32)
    n_used = pend[-1] // bm
    blk_v = (blk < n_used).astype(jnp.int32)
    blk_x = jnp.minimum(blk, n_used - 1)
    blk_e = jnp.minimum(jnp.searchsorted(pend, blk_x * bm, side='right'), N_EXPERTS - 1).astype(jnp.int32)
    xs = jnp.take(h2, slot_tok, axis=0)
    ys = _moe_call(xs, blk_e, blk_x, blk_v, w_gu, b_gu, w_down, b_down, bm)
    yg = jnp.take(ys, inv_slot.reshape(n, TOP_K).T.reshape(-1), axis=0).reshape(TOP_K, n, d)
    return _combine_call(x1, yg, gates, g2, rows_per_mod, final_g, final)


def _conv_kernel(prev_ref, cur_ref, next_ref, w_ref, bdw_ref, lg_ref, lb_ref, o_ref, buf_ref, cv_ref):
    i = pl.program_id(1)
    n_i = pl.num_programs(1)
    tr = cur_ref.shape[0]
    d = cur_ref.shape[1]
    hl = CONV_HALO
    pad = CONV_KERNEL // 2
    buf_ref[0:hl, :] = jnp.where(i > 0, prev_ref[...].astype(F32), 0.0)
    buf_ref[hl:hl + tr, :] = cur_ref[...].astype(F32)
    buf_ref[hl + tr:, :] = jnp.where(i < n_i - 1, next_ref[...].astype(F32), 0.0)
    s1 = jnp.zeros((tr, LANES), F32)
    for cb in range(d // LANES):
        ls = slice(cb * LANES, (cb + 1) * LANES)
        acc = jnp.zeros((tr, LANES), F32) + bdw_ref[:, ls]
        for k in range(CONV_KERNEL):
            off = hl - pad + k
            acc = acc + w_ref[k:k + 1, ls] * buf_ref[off:off + tr, ls]
        cv_ref[:, ls] = acc
        s1 = s1 + acc
    mu = jnp.sum(s1, axis=-1, keepdims=True) * (1.0 / d)
    xc = cv_ref[...] - mu
    var = jnp.mean(xc * xc, axis=-1, keepdims=True)
    y = xc * lax.rsqrt(var + NORM_EPS) * lg_ref[...] + lb_ref[...]
    o_ref[...] = (y * jax.nn.sigmoid(y)).astype(o_ref.dtype)


def _conv_call(a3, w_dw, b_dw, ln_g, ln_b, tr=256):
    b, t, d = a3.shape
    tr = min(tr, t)
    hl = CONV_HALO
    rb = tr // hl
    n_hb = t // hl
    kpad = -(-CONV_KERNEL // 8) * 8
    w = jnp.zeros((kpad, d), F32).at[:CONV_KERNEL].set(w_dw)
    vec = lambda v: v.reshape(1, d)
    full = lambda shape: pl.BlockSpec(shape, lambda bi, i: (0, 0))
    return pl.pallas_call(
        _conv_kernel,
        grid=(b, t // tr),
        in_specs=[pl.BlockSpec((None, hl, d), lambda bi, i: (bi, jnp.maximum(i * rb - 1, 0), 0)),
                  pl.BlockSpec((None, tr, d), lambda bi, i: (bi, i, 0)),
                  pl.BlockSpec((None, hl, d), lambda bi, i: (bi, jnp.minimum((i + 1) * rb, n_hb - 1), 0)),
                  full((kpad, d)), full((1, d)), full((1, d)), full((1, d))],
        out_specs=pl.BlockSpec((None, tr, d), lambda bi, i: (bi, i, 0)),
        out_shape=jax.ShapeDtypeStruct((b, t, d), BF16),
        scratch_shapes=[pltpu.VMEM((tr + 2 * hl, d), F32), pltpu.VMEM((tr, d), F32)],
        compiler_params=_cparams(("parallel", "parallel")),
        name="dwconv_ln_silu",
    )(a3, a3, a3, w, vec(b_dw), vec(ln_g), vec(ln_b))


def _forward(x, c, ctx, c_ctx, norm_mix_g, norm_ffn_g, ada_w, ada_b,
             ev_w_in, ev_w_out, ev_hgrn_lb_logits, ev_hgrn_norm_g, ev_dattn_lambda, ev_dattn_subln_g,
             od_w_pw1, od_b_pw1, od_w_dw, od_b_dw, od_ln_g, od_ln_b, od_w_pw2, od_b_pw2,
             moe_w_router, moe_b_router, moe_w_gu, moe_b_gu, moe_w_down, moe_b_down, final_norm_g,
             moe_bm=1024):
    b, t, d = x.shape
    cl = ctx.shape[1]
    depth = ada_w.shape[0]
    n_hh = (d // 2) // HEAD
    n_ah = (d // 2) // HEAD
    n = b * t

    n_cond = -(-(b + 1) // 8) * 8
    cs = jnp.zeros((n_cond, d), F32).at[:b].set(c).at[b].set(c_ctx)
    mod = _ada_call(cs, ada_w, ada_b)

    x2 = x.reshape(n, d)
    ctx2 = ctx.reshape(b * cl, d)
    for l in range(depth):
        j = l // 2
        parts = [mod[l, :, i * d:(i + 1) * d] for i in range(N_MOD)]
        sh1, sc1, g1, sh2, sc2, g2 = [p[:b].reshape(b, 1, d) for p in parts]
        csh1, csc1 = [p[b:b + 1].reshape(1, 1, d) for p in parts[:2]]
        gmix = norm_mix_g[l].reshape(1, d)
        gffn = norm_ffn_g[l].reshape(1, d)
        if l % 2 == 0:
            w_in = ev_w_in[j].astype(BF16)
            p = _proj_call(x2, gmix, sh1, sc1, t, w_in)
            pc = _proj_call(ctx2, gmix, csh1, csc1, b * cl, w_in)
            p3 = p.reshape(b, t, p.shape[1])
            pc3 = pc.reshape(b, cl, pc.shape[1])
            p_lb = jax.nn.softmax(ev_hgrn_lb_logits.astype(F32), axis=1)
            lb = jnp.cumsum(p_lb, axis=1)[:, j].reshape(2, n_hh, 1, HEAD)
            yh = _hgrn_call(p3, pc3, lb, ev_hgrn_norm_g[j].reshape(1, HEAD), n_hh)
            lam_init = 0.8 - 0.6 * math.exp(-0.3 * l)
            lp = ev_dattn_lambda[j].astype(F32)
            lam = (jnp.exp(jnp.sum(lp[0] * lp[1])) - jnp.exp(jnp.sum(lp[2] * lp[3])) + lam_init).reshape(1)
            ya = _attn_call(p3, pc3, lam, ev_dattn_subln_g[j].reshape(1, HEAD), n_hh, n_ah, 1.0 - lam_init)
            w_out = ev_w_out[j].astype(BF16)
            hw = n_hh * HEAD
            acts = [yh.reshape(n, hw), ya.reshape(n, n_ah * HEAD)]
            ws = [w_out[:hw], w_out[hw:]]
            bias = None
        else:
            a = _proj_call(x2, gmix, sh1, sc1, t, od_w_pw1[j].astype(BF16), glu_bias=od_b_pw1[j])
            a = _conv_call(a.reshape(b, t, d), od_w_dw[j], od_b_dw[j], od_ln_g[j], od_ln_b[j])
            acts = [a.reshape(n, d)]
            ws = [od_w_pw2[j].astype(BF16)]
            bias = od_b_pw2[j]
        x1, h2, topi, gates = _resid_router_call(acts, ws, bias, x2, g1, gffn, sh2, sc2, t,
                                                 moe_w_router[l], moe_b_router[l])
        x2 = _moe_layer(x1, h2, topi, gates, g2, t, moe_w_gu[l].astype(BF16), moe_b_gu[l],
                        moe_w_down[l].astype(BF16), moe_b_down[l], final_norm_g, l == depth - 1, moe_bm)
    return x2.reshape(b, t, d)


def kernel(x, c, ctx, c_ctx, norm_mix_g, norm_ffn_g, ada_w, ada_b, ev_w_in, ev_w_out, ev_hgrn_lb_logits, ev_hgrn_norm_g, ev_dattn_lambda, ev_dattn_subln_g, od_w_pw1, od_b_pw1, od_w_dw, od_b_dw, od_ln_g, od_ln_b, od_w_pw2, od_b_pw2, moe_w_router, moe_b_router, moe_w_gu, moe_b_gu, moe_w_down, moe_b_down, final_norm_g):
    return _forward(x, c, ctx, c_ctx, norm_mix_g, norm_ffn_g, ada_w, ada_b, ev_w_in, ev_w_out,
                    ev_hgrn_lb_logits, ev_hgrn_norm_g, ev_dattn_lambda, ev_dattn_subln_g,
                    od_w_pw1, od_b_pw1, od_w_dw, od_b_dw, od_ln_g, od_ln_b, od_w_pw2, od_b_pw2,
                    moe_w_router, moe_b_router, moe_w_gu, moe_b_gu, moe_w_down, moe_b_down, final_norm_g)
```

```python
import functools
import math

import numpy as np
import jax
import jax.numpy as jnp
from jax import lax
from jax.experimental import pallas as pl
from jax.experimental.pallas import tpu as pltpu
from jax.experimental.pallas import tpu_sc as plsc

F32 = jnp.float32
BF16 = jnp.bfloat16

NORM_EPS = 1e-6
GRID_W = 64
N_MOD = 6
HEAD = 128
DATTN_HEAD_DIM = 64
ROPE_BASE = 10000.0
CONV_KERNEL = 31
CONV_HALO = 16
N_EXPERTS = 32
TOP_K = 4
SWIGLU_ALPHA = 1.702
SWIGLU_LIMIT = 7.0
LANES = 128
HGRN_CHUNK = 128
NEG_BIG = -1e30
SC_CORES = 2
SC_SUBCORES = 16
SC_GATHER_ROWS = 32
HI_MASK = 0xFFFF0000

VMEM_LIMIT = 56 * 1024 * 1024


def _cparams(sem):
    return pltpu.CompilerParams(dimension_semantics=sem, vmem_limit_bytes=VMEM_LIMIT)


def _dot(a, b):
    return jnp.dot(a, b, preferred_element_type=F32)


def _dot_nt(a, b):
    return lax.dot_general(a, b, (((1,), (1,)), ((), ())), preferred_element_type=F32)


def _dot_tn(a, b):
    return lax.dot_general(a, b, (((0,), (0,)), ((), ())), preferred_element_type=F32)


def _split3(x):
    hi = x.astype(BF16)
    r1 = x - hi.astype(F32)
    mid = r1.astype(BF16)
    lo = (r1 - mid.astype(F32)).astype(BF16)
    return hi, mid, lo


def _norm_mod(x, g, shift, scale):
    ms = jnp.mean(x * x, axis=-1, keepdims=True)
    return (x * lax.rsqrt(ms + NORM_EPS) * g) * (1.0 + scale) + shift


def _pack_halves(a):
    half = a.shape[1] // 2
    lo = pltpu.bitcast(a[:, :half], jnp.uint32) >> 16
    hi = pltpu.bitcast(a[:, half:], jnp.uint32) & jnp.uint32(HI_MASK)
    return hi | lo


def _unpack_halves(p):
    lo = pltpu.bitcast(p << 16, F32)
    hi = pltpu.bitcast(p & jnp.uint32(HI_MASK), F32)
    return lo, hi


def _sc_gather(table, idx):
    m = idx.shape[0]
    w = table.shape[1]
    r = SC_GATHER_ROWS
    n_workers = SC_CORES * SC_SUBCORES
    per_worker = m // n_workers
    n_steps = per_worker // r
    assert per_worker * n_workers == m and n_steps * r == per_worker and n_steps % 2 == 0 and n_steps >= 2
    mesh = plsc.VectorSubcoreMesh(core_axis_name="c", subcore_axis_name="s")

    @functools.partial(
        pl.kernel, mesh=mesh,
        out_type=jax.ShapeDtypeStruct((m, w), table.dtype),
        scratch_types=[pltpu.VMEM((per_worker,), jnp.int32), pltpu.VMEM((2, r, w), table.dtype),
                       pltpu.SemaphoreType.DMA((2,)), pltpu.SemaphoreType.DMA((2,))],
    )
    def gather_kernel(t_hbm, i_hbm, o_hbm, idx_v, rows_v, gsem, osem):
        base = (lax.axis_index("s") * SC_CORES + lax.axis_index("c")) * per_worker
        pltpu.sync_copy(i_hbm.at[pl.ds(base, per_worker)], idx_v)

        def fetch(j, b):
            return pltpu.make_async_copy(t_hbm.at[idx_v.at[pl.ds(j * r, r)]], rows_v.at[b], gsem.at[b])

        def flush(j, b):
            return pltpu.make_async_copy(rows_v.at[b], o_hbm.at[pl.ds(base + j * r, r)], osem.at[b])

        fetch(0, 0).start()

        @pl.loop(0, n_steps, step=2)
        def _(j0):
            for b in range(2):
                j = j0 + b
                fetch(j, b).wait()
                flush(j, b).start()

                @pl.when(j + 1 < n_steps)
                def _():
                    @pl.when(j >= 1)
                    def _():
                        flush(j - 1, 1 - b).wait()

                    fetch(j + 1, 1 - b).start()

        flush(n_steps - 2, 0).wait()
        flush(n_steps - 1, 1).wait()

    return gather_kernel(table, idx)


def _ada_kernel(cs_ref, w_ref, b_ref, o_ref):
    cs = cs_ref[...]
    s = cs * jax.nn.sigmoid(cs)
    hi, mid, lo = _split3(s)
    wh, wm, wl = _split3(w_ref[...])
    acc = _dot(hi, wh) + (_dot(hi, wm) + _dot(mid, wh)) + (_dot(hi, wl) + _dot(mid, wm) + _dot(lo, wh))
    o_ref[...] = acc + b_ref[...]


def _ada_call(cs, ada_w, ada_b):
    n_layers, d, n6 = ada_w.shape
    r = cs.shape[0]
    tn = min(512, n6)
    return pl.pallas_call(
        _ada_kernel,
        grid=(n_layers, n6 // tn),
        in_specs=[
            pl.BlockSpec((r, d), lambda l, j: (0, 0)),
            pl.BlockSpec((None, d, tn), lambda l, j: (l, 0, j)),
            pl.BlockSpec((None, 1, tn), lambda l, j: (l, 0, j)),
        ],
        out_specs=pl.BlockSpec((None, r, tn), lambda l, j: (l, 0, j)),
        out_shape=jax.ShapeDtypeStruct((n_layers, r, n6), F32),
        compiler_params=_cparams(("parallel", "parallel")),
        name="ada_mod",
    )(cs, ada_w, ada_b.reshape(n_layers, 1, n6))


def _proj_kernel(x_ref, g_ref, sh_ref, sc_ref, w_ref, o_ref, h_ref):
    @pl.when(pl.program_id(1) == 0)
    def _():
        h_ref[...] = _norm_mod(x_ref[...], g_ref[...], sh_ref[...], sc_ref[...]).astype(BF16)

    o_ref[...] = _dot(h_ref[...], w_ref[...]).astype(o_ref.dtype)


def _proj_glu_kernel(x_ref, g_ref, sh_ref, sc_ref, wa_ref, wb_ref, ba_ref, bb_ref, o_ref, h_ref):
    @pl.when(pl.program_id(1) == 0)
    def _():
        h_ref[...] = _norm_mod(x_ref[...], g_ref[...], sh_ref[...], sc_ref[...]).astype(BF16)

    h = h_ref[...]
    a = _dot(h, wa_ref[...]) + ba_ref[...]
    b = _dot(h, wb_ref[...]) + bb_ref[...]
    o_ref[...] = (a * jax.nn.sigmoid(b)).astype(o_ref.dtype)


def _proj_call(x2, g, shift, scale, rows_per_mod, w, glu_bias=None, tm=1024, tn=1024):
    m, d = x2.shape
    n_out = w.shape[1] if glu_bias is None else w.shape[1] // 2
    tm = min(tm, m, rows_per_mod)
    tn = min(tn, n_out)
    bpm = rows_per_mod // tm
    row_spec = pl.BlockSpec((tm, d), lambda i, j: (i, 0))
    vec_spec = pl.BlockSpec((1, d), lambda i, j: (0, 0))
    mod_spec = pl.BlockSpec((None, 1, d), lambda i, j: (i // bpm, 0, 0))
    out_spec = pl.BlockSpec((tm, tn), lambda i, j: (i, j))
    common = dict(
        grid=(m // tm, n_out // tn),
        out_specs=out_spec,
        out_shape=jax.ShapeDtypeStruct((m, n_out), BF16),
        scratch_shapes=[pltpu.VMEM((tm, d), BF16)],
        compiler_params=_cparams(("parallel", "arbitrary")),
    )
    if glu_bias is None:
        return pl.pallas_call(
            _proj_kernel,
            in_specs=[row_spec, vec_spec, mod_spec, mod_spec,
                      pl.BlockSpec((d, tn), lambda i, j: (0, j))],
            name="norm_proj",
            **common,
        )(x2, g, shift, scale, w)
    nb = n_out // tn
    bias = glu_bias.reshape(1, 2 * n_out)
    return pl.pallas_call(
        _proj_glu_kernel,
        in_specs=[row_spec, vec_spec, mod_spec, mod_spec,
                  pl.BlockSpec((d, tn), lambda i, j: (0, j)),
                  pl.BlockSpec((d, tn), lambda i, j: (0, j + nb)),
                  pl.BlockSpec((1, tn), lambda i, j: (0, j)),
                  pl.BlockSpec((1, tn), lambda i, j: (0, j + nb))],
        name="norm_proj_glu",
        **common,
    )(x2, g, shift, scale, w, w, bias, bias)


def _hgrn_tables(c):
    n_lvl = int(math.log2(c))
    t = np.arange(c)[:, None]
    s = np.arange(c)[None, :]
    gs = [(s <= t).astype(np.float32)]
    masks = [(s == t).astype(np.float32)]
    for lvl in range(1, n_lvl + 1):
        b = 2 ** (lvl - 1)
        mid = (t // (2 * b)) * (2 * b) + b - 1
        upper = t > mid
        g = np.where(upper & (s > mid) & (s <= t), 1.0, 0.0) - np.where((~upper) & (s > t) & (s <= mid), 1.0, 0.0)
        same = (t // (2 * b)) == (s // (2 * b))
        gs.append(g.astype(np.float32))
        masks.append((same & upper & (s <= mid)).astype(np.float32))
    gs_f = np.concatenate(gs, axis=0)
    mk_f = np.stack(masks, axis=0)
    gs_b = np.concatenate([g[::-1, ::-1] for g in gs], axis=0)
    mk_b = np.stack([m[::-1, ::-1] for m in masks], axis=0)
    return np.stack([gs_f, gs_b]), np.stack([mk_f, mk_b])


def _hgrn_gate(z, lb):
    e = jnp.exp(-jnp.abs(z))
    r = 1.0 / (1.0 + e)
    er = e * r
    pos = z >= 0
    sig = jnp.where(pos, r, er)
    sig_neg = jnp.where(pos, er, r)
    logf = jnp.log(lb + (1.0 - lb) * sig)
    return logf, (1.0 - lb) * sig_neg


def _gs_dot(gs, logf):
    hi, mid, lo = _split3(logf)
    return _dot(gs, hi) + _dot(gs, mid) + _dot(gs, lo)


def _hgrn_state_step(z, v, lb, tri, last_row, st):
    logf, k = _hgrn_gate(z, lb)
    cum = _gs_dot(tri, logf)
    tot = cum[last_row:last_row + 1, :]
    kt = (k * jnp.exp(tot - cum)).astype(BF16)
    return st * jnp.exp(tot) + _dot_tn(v, kt)


def _hgrn_chunk_step(q, z, v, lb, gs_ref, mk_ref, d, last_row, st):
    c = q.shape[0]
    n_lvl = mk_ref.shape[1] - 1
    logf, k = _hgrn_gate(z, lb)
    dist = _gs_dot(gs_ref[d], logf)
    cum = dist[0:c]
    tot = cum[last_row:last_row + 1, :]
    qf = q.astype(F32)
    att = mk_ref[d, 0] * _dot_nt(q, k.astype(BF16))
    for lvl in range(1, n_lvl + 1):
        e = jnp.exp(-jnp.abs(dist[lvl * c:(lvl + 1) * c]))
        att = att + mk_ref[d, lvl] * _dot_nt((qf * e).astype(BF16), (k * e).astype(BF16))
    o = _dot(att.astype(BF16), v) + _dot_nt((qf * jnp.exp(cum)).astype(BF16), st.astype(BF16))
    kt = (k * jnp.exp(tot - cum)).astype(BF16)
    st_new = st * jnp.exp(tot) + _dot_tn(v, kt)
    return o, st_new


def _hgrn_kernel(q_ref, v_ref, zf_ref, zb_ref, g_ref, vc_ref, zfc_ref, zbc_ref, lb_ref, gn_ref,
                 gs_ref, mk_ref, o_ref, acc_ref):
    c = HGRN_CHUNK
    t_len = q_ref.shape[0]
    n = t_len // c
    nc = vc_ref.shape[0] // c
    lbf = lb_ref[0]
    lbb = lb_ref[1]
    acc_ref[...] = jnp.zeros_like(acc_ref)
    st0 = jnp.zeros((HEAD, HEAD), F32)

    def rows(i):
        return pl.ds(pl.multiple_of(i * c, c), c)

    def ctx_body(i, carry):
        sf, sb = carry
        rf = rows(i)
        rb = rows(nc - 1 - i)
        sf = _hgrn_state_step(zfc_ref[rf, :].astype(F32), vc_ref[rf, :], lbf, gs_ref[0, 0:c, :], c - 1, sf)
        sb = _hgrn_state_step(zbc_ref[rb, :].astype(F32), vc_ref[rb, :], lbb, gs_ref[1, 0:c, :], 0, sb)
        return sf, sb

    sf, sb = lax.fori_loop(0, nc, ctx_body, (st0, st0))

    def body(i, carry):
        sf, sb = carry
        rf = rows(i)
        rb = rows(n - 1 - i)
        of, sf = _hgrn_chunk_step(q_ref[rf, :], zf_ref[rf, :].astype(F32), v_ref[rf, :], lbf,
                                  gs_ref, mk_ref, 0, c - 1, sf)
        acc_ref[rf, :] += of
        ob, sb = _hgrn_chunk_step(q_ref[rb, :], zb_ref[rb, :].astype(F32), v_ref[rb, :], lbb,
                                  gs_ref, mk_ref, 1, 0, sb)
        acc_ref[rb, :] += ob
        return sf, sb

    lax.fori_loop(0, n, body, (sf, sb))

    o = acc_ref[...]
    gate = g_ref[...].astype(F32)
    y = o * lax.rsqrt(jnp.mean(o * o, axis=-1, keepdims=True) + NORM_EPS) * gn_ref[...]
    o_ref[...] = (y * (gate * jax.nn.sigmoid(gate))).astype(o_ref.dtype)


def _hgrn_call(p3, pc3, lb, gn, n_heads):
    b, t, _ = p3.shape
    cl = pc3.shape[1]
    h = n_heads
    gs, mk = _hgrn_tables(HGRN_CHUNK)
    gs = jnp.asarray(gs, BF16)
    mk = jnp.asarray(mk, F32)

    def col(seg, rows_):
        return pl.BlockSpec((None, rows_, HEAD), lambda bi, hi, seg=seg: (bi, 0, seg * h + hi))

    return pl.pallas_call(
        _hgrn_kernel,
        grid=(b, h),
        in_specs=[col(0, t), col(1, t), col(2, t), col(3, t), col(4, t),
                  col(1, cl), col(2, cl), col(3, cl),
                  pl.BlockSpec((2, None, 1, HEAD), lambda bi, hi: (0, hi, 0, 0)),
                  pl.BlockSpec((1, HEAD), lambda bi, hi: (0, 0)),
                  pl.BlockSpec(gs.shape, lambda bi, hi: (0, 0, 0)),
                  pl.BlockSpec(mk.shape, lambda bi, hi: (0, 0, 0, 0))],
        out_specs=pl.BlockSpec((None, t, HEAD), lambda bi, hi: (bi, 0, hi)),
        out_shape=jax.ShapeDtypeStruct((b, t, h * HEAD), BF16),
        scratch_shapes=[pltpu.VMEM((t, HEAD), F32)],
        compiler_params=_cparams(("parallel", "parallel")),
        name="hgrn2",
    )(p3, p3, p3, p3, p3, pc3, pc3, pc3, lb, gn, gs, mk)


def _rope_tables(t_len):
    rows = t_len // GRID_W
    row = jnp.repeat(jnp.arange(rows, dtype=F32), GRID_W)
    colp = jnp.tile(jnp.arange(GRID_W, dtype=F32), rows)
    n_freq = DATTN_HEAD_DIM // 4
    inv = ROPE_BASE ** (-jnp.arange(n_freq, dtype=F32) / n_freq)
    ang_r = row[:, None] * inv
    ang_c = colp[:, None] * inv
    cos64 = jnp.concatenate([jnp.cos(ang_r), jnp.cos(ang_r), jnp.cos(ang_c), jnp.cos(ang_c)], axis=1)
    sin64 = jnp.concatenate([-jnp.sin(ang_r), jnp.sin(ang_r), -jnp.sin(ang_c), jnp.sin(ang_c)], axis=1)
    lane = np.arange(HEAD)
    partner = np.where(lane % (2 * n_freq) < n_freq, lane + n_freq, lane - n_freq)
    perm = np.zeros((HEAD, HEAD), np.float32)
    perm[partner, lane] = 1.0
    return jnp.tile(cos64, (1, 2)), jnp.tile(sin64, (1, 2)), jnp.asarray(perm, BF16)


def _attn_kernel(lam_ref, q_ref, k_ref, v_ref, kc_ref, vc_ref, cos_ref, sin_ref, perm_ref, sg_ref,
                 o_ref, kall_ref, vall_ref, *, out_scale):
    qi = pl.program_id(2)
    tq = q_ref.shape[0]
    cl = kc_ref.shape[0]

    @pl.when(qi == 0)
    def _():
        k = k_ref[...]
        kr = k.astype(F32) * cos_ref[...] + _dot(k, perm_ref[...]) * sin_ref[...]
        kall_ref[cl:, :] = kr.astype(BF16)
        kall_ref[0:cl, :] = kc_ref[...]
        vall_ref[cl:, 0:HEAD] = v_ref[...]
        vall_ref[0:cl, 0:HEAD] = vc_ref[...]
        vall_ref[:, HEAD:] = jnp.ones((vall_ref.shape[0], HEAD), BF16)

    r0 = pl.multiple_of(qi * tq, tq)
    q = q_ref[...]
    cq = cos_ref[pl.ds(r0, tq), :]
    sq = sin_ref[pl.ds(r0, tq), :]
    qr = (q.astype(F32) * cq + _dot(q, perm_ref[...]) * sq) * (DATTN_HEAD_DIM ** -0.5)
    lane = lax.broadcasted_iota(jnp.int32, (tq, HEAD), 1)
    q0 = jnp.where(lane < DATTN_HEAD_DIM, qr, 0.0).astype(BF16)
    q1 = jnp.where(lane >= DATTN_HEAD_DIM, qr, 0.0).astype(BF16)
    lam = lam_ref[0]

    def softmax_av(qm):
        s = _dot_nt(qm, kall_ref[...])
        m = jnp.max(s, axis=-1, keepdims=True)
        p = jnp.exp((s - m).astype(BF16))
        oa = _dot(p, vall_ref[...])
        return oa[:, 0:HEAD] / oa[:, HEAD:]

    o = softmax_av(q0) - lam * softmax_av(q1)
    y = o * lax.rsqrt(jnp.mean(o * o, axis=-1, keepdims=True) + NORM_EPS) * sg_ref[...] * out_scale
    o_ref[...] = y.astype(o_ref.dtype)


def _attn_call(p3, pc3, lam, subln_g, n_hgrn_heads, n_heads, out_scale, tq=256):
    b, t, _ = p3.shape
    cl = pc3.shape[1]
    tq = min(tq, t)
    base = 5 * n_hgrn_heads
    cos_t, sin_t, perm = _rope_tables(t)

    def col(seg, rows_):
        return pl.BlockSpec((None, rows_, HEAD), lambda bi, hi, qi, seg=seg: (bi, 0, base + seg * n_heads + hi))

    full2 = lambda shape: pl.BlockSpec(shape, lambda bi, hi, qi: (0, 0))
    return pl.pallas_call(
        functools.partial(_attn_kernel, out_scale=out_scale),
        grid=(b, n_heads, t // tq),
        in_specs=[pl.BlockSpec(memory_space=pltpu.SMEM),
                  pl.BlockSpec((None, tq, HEAD), lambda bi, hi, qi: (bi, qi, base + hi)),
                  col(1, t), col(2, t), col(1, cl), col(2, cl),
                  full2((t, HEAD)), full2((t, HEAD)), full2((HEAD, HEAD)), full2((1, HEAD))],
        out_specs=pl.BlockSpec((None, tq, HEAD), lambda bi, hi, qi: (bi, qi, hi)),
        out_shape=jax.ShapeDtypeStruct((b, t, n_heads * HEAD), BF16),
        scratch_shapes=[pltpu.VMEM((cl + t, HEAD), BF16), pltpu.VMEM((cl + t, 2 * HEAD), BF16)],
        compiler_params=_cparams(("parallel", "parallel", "arbitrary")),
        name="diff_attn",
    )(lam, p3, p3, p3, pc3, pc3, cos_t, sin_t, perm, subln_g)


def _resid_router_kernel(*refs, n_act, has_bias):
    acts = refs[0:n_act]
    ws = refs[n_act:2 * n_act]
    pos = 2 * n_act
    bias_ref = refs[pos] if has_bias else None
    pos += int(has_bias)
    x_ref, g1_ref, gf_ref, sh_ref, sc_ref, wrh_ref, wrl_ref, br_ref = refs[pos:pos + 8]
    x1_ref, h2_ref, ti_ref, gt_ref = refs[pos + 8:pos + 12]

    y = _dot(acts[0][...], ws[0][...])
    for a, w in zip(acts[1:], ws[1:]):
        y = y + _dot(a[...], w[...])
    if has_bias:
        y = y + bias_ref[...]
    x1 = x_ref[...] + g1_ref[...] * y
    x1_ref[...] = x1
    h2 = _norm_mod(x1, gf_ref[...], sh_ref[...], sc_ref[...])
    hi = h2.astype(BF16)
    hi_f = hi.astype(F32)
    h2_ref[...] = _pack_halves(hi_f)
    lo = (h2 - hi_f).astype(BF16)
    wrh = wrh_ref[...]
    logits = _dot(hi, wrh) + (_dot(lo, wrh) + _dot(hi, wrl_ref[...])) + br_ref[...]

    tm = logits.shape[0]
    lane = lax.broadcasted_iota(jnp.int32, (tm, LANES), 1).astype(F32)
    vals, idxs = [], []
    rem = logits
    for _ in range(TOP_K):
        m = jnp.max(rem, axis=-1, keepdims=True)
        idx = jnp.min(jnp.where(rem == m, lane, float(LANES)), axis=-1, keepdims=True)
        vals.append(m)
        idxs.append(idx)
        rem = jnp.where(lane == idx, -jnp.inf, rem)
    es = [jnp.exp(v - vals[0]) for v in vals]
    inv = 1.0 / (es[0] + es[1] + es[2] + es[3])
    ti = jnp.zeros((tm, LANES), F32)
    gt = jnp.zeros((tm, LANES), F32)
    for k in range(TOP_K):
        ti = jnp.where(lane == float(k), idxs[k], ti)
        gt = jnp.where(lane == float(k), es[k] * inv, gt)
    ti_ref[...] = ti.astype(jnp.int32)
    gt_ref[...] = gt


def _resid_router_call(acts, ws, bias, x2, g1, gf, sh2, sc2, rows_per_mod, w_router, b_router, tm=256):
    m, d = x2.shape
    tm = min(tm, m, rows_per_mod)
    bpm = rows_per_mod // tm
    n_act = len(acts)
    wr = jnp.zeros((d, LANES), F32).at[:, :N_EXPERTS].set(w_router)
    wr_hi = wr.astype(BF16)
    wr_lo = (wr - wr_hi.astype(F32)).astype(BF16)
    br = jnp.full((1, LANES), NEG_BIG, F32).at[0, :N_EXPERTS].set(b_router)
    row = lambda width: pl.BlockSpec((tm, width), lambda i: (i, 0))
    full = lambda shape: pl.BlockSpec(shape, lambda i: (0, 0))
    mod_spec = pl.BlockSpec((None, 1, d), lambda i: (i // bpm, 0, 0))
    in_specs = [row(a.shape[1]) for a in acts] + [full(w.shape) for w in ws]
    args = list(acts) + list(ws)
    if bias is not None:
        in_specs.append(full((1, d)))
        args.append(bias.reshape(1, d))
    in_specs += [row(d), mod_spec, full((1, d)), mod_spec, mod_spec,
                 full((d, LANES)), full((d, LANES)), full((1, LANES))]
    args += [x2, g1, gf, sh2, sc2, wr_hi, wr_lo, br]
    return pl.pallas_call(
        functools.partial(_resid_router_kernel, n_act=n_act, has_bias=bias is not None),
        grid=(m // tm,),
        in_specs=in_specs,
        out_specs=[row(d), row(d // 2), row(LANES), row(LANES)],
        out_shape=[jax.ShapeDtypeStruct((m, d), F32), jax.ShapeDtypeStruct((m, d // 2), jnp.uint32),
                   jax.ShapeDtypeStruct((m, LANES), jnp.int32), jax.ShapeDtypeStruct((m, LANES), F32)],
        compiler_params=_cparams(("parallel",)),
        name="resid_router",
    )(*args)


def _moe_kernel(be_ref, bx_ref, bv_ref, x_ref, wg_ref, wu_ref, bg_ref, bu_ref, wd_ref, bd_ref, o_ref,
                xb_ref, acc_ref):
    i = pl.program_id(0)
    f = pl.program_id(1)
    last = pl.num_programs(1) - 1
    valid = bv_ref[i] > 0
    half = x_ref.shape[1]

    @pl.when(jnp.logical_and(valid, f == 0))
    def _():
        lo, hi = _unpack_halves(x_ref[...])
        xb_ref[:, 0:half] = lo.astype(BF16)
        xb_ref[:, half:] = hi.astype(BF16)

    @pl.when(valid)
    def _():
        x = xb_ref[...]
        g = _dot(x, wg_ref[...].astype(BF16)) + bg_ref[...]
        u = _dot(x, wu_ref[...].astype(BF16)) + bu_ref[...]
        g = jnp.minimum(g, SWIGLU_LIMIT)
        u = jnp.clip(u, -SWIGLU_LIMIT, SWIGLU_LIMIT)
        a = g * jax.nn.sigmoid(SWIGLU_ALPHA * g) * (u + 1.0)
        contrib = _dot(a.astype(BF16), wd_ref[...].astype(BF16))

        @pl.when(f == 0)
        def _():
            acc_ref[...] = contrib

        @pl.when(f > 0)
        def _():
            acc_ref[...] += contrib

        @pl.when(f == last)
        def _():
            y = (acc_ref[...] + bd_ref[...]).astype(BF16).astype(F32)
            o_ref[...] = _pack_halves(y)

    @pl.when(jnp.logical_and(jnp.logical_not(valid), f == last))
    def _():
        o_ref[...] = jnp.zeros_like(o_ref)


def _moe_call(xs, blk_e, blk_x, blk_v, layer, w_gu, b_gu, w_down, b_down, bm, tf=256):
    n_slots, half = xs.shape
    d = 2 * half
    n_l, n_e, ff = w_down.shape[0:3]
    tf = min(tf, ff)
    nf = ff // tf
    nb = n_slots // bm
    fsel = lambda f, bv, i: jnp.where(bv[i] > 0, f, nf - 1)
    grid_spec = pltpu.PrefetchScalarGridSpec(
        num_scalar_prefetch=3,
        grid=(nb, nf),
        in_specs=[
            pl.BlockSpec((bm, half), lambda i, f, be, bx, bv: (bx[i], 0)),
            pl.BlockSpec((None, None, d, tf), lambda i, f, be, bx, bv: (layer, be[i], 0, fsel(f, bv, i))),
            pl.BlockSpec((None, None, d, tf), lambda i, f, be, bx, bv: (layer, be[i], 0, fsel(f, bv, i) + nf)),
            pl.BlockSpec((None, None, 1, tf), lambda i, f, be, bx, bv: (layer, be[i], 0, fsel(f, bv, i))),
            pl.BlockSpec((None, None, 1, tf), lambda i, f, be, bx, bv: (layer, be[i], 0, fsel(f, bv, i) + nf)),
            pl.BlockSpec((None, None, tf, d), lambda i, f, be, bx, bv: (layer, be[i], fsel(f, bv, i), 0)),
            pl.BlockSpec((None, None, 1, d), lambda i, f, be, bx, bv: (layer, be[i], 0, 0)),
        ],
        out_specs=pl.BlockSpec((bm, half), lambda i, f, be, bx, bv: (i, 0)),
        scratch_shapes=[pltpu.VMEM((bm, d), BF16), pltpu.VMEM((bm, d), F32)],
    )
    b_gu4 = b_gu.reshape(n_l, n_e, 1, 2 * ff)
    return pl.pallas_call(
        _moe_kernel,
        grid_spec=grid_spec,
        out_shape=jax.ShapeDtypeStruct((n_slots, half), jnp.uint32),
        compiler_params=_cparams(("arbitrary", "arbitrary")),
        name="moe_experts",
    )(blk_e, blk_x, blk_v, xs, w_gu, w_gu, b_gu4, b_gu4, w_down, b_down.reshape(n_l, n_e, 1, d))


def _combine_kernel(x_ref, yg_ref, gt_ref, g2_ref, fg_ref, o_ref, *, final):
    gt = gt_ref[...]
    acc_lo, acc_hi = None, None
    for k in range(TOP_K):
        lo, hi = _unpack_halves(yg_ref[k])
        gk = gt[:, k:k + 1]
        acc_lo = lo * gk if acc_lo is None else acc_lo + lo * gk
        acc_hi = hi * gk if acc_hi is None else acc_hi + hi * gk
    acc = jnp.concatenate([acc_lo, acc_hi], axis=1)
    x2 = x_ref[...] + g2_ref[...] * acc
    if final:
        x2 = x2 * lax.rsqrt(jnp.mean(x2 * x2, axis=-1, keepdims=True) + NORM_EPS) * fg_ref[...]
    o_ref[...] = x2


def _combine_call(x1, yg, gates, g2, rows_per_mod, final_g, final, tm=256):
    m, d = x1.shape
    tm = min(tm, m, rows_per_mod)
    bpm = rows_per_mod // tm
    return pl.pallas_call(
        functools.partial(_combine_kernel, final=final),
        grid=(m // tm,),
        in_specs=[pl.BlockSpec((tm, d), lambda i: (i, 0)),
                  pl.BlockSpec((TOP_K, tm, d // 2), lambda i: (0, i, 0)),
                  pl.BlockSpec((tm, LANES), lambda i: (i, 0)),
                  pl.BlockSpec((None, 1, d), lambda i: (i // bpm, 0, 0)),
                  pl.BlockSpec((1, d), lambda i: (0, 0))],
        out_specs=pl.BlockSpec((tm, d), lambda i: (i, 0)),
        out_shape=jax.ShapeDtypeStruct((m, d), F32),
        compiler_params=_cparams(("parallel",)),
        name="moe_combine",
    )(x1, yg, gates, g2, final_g.reshape(1, d))


def _moe_layer(x1, h2, topi, gates, g2, rows_per_mod, layer, w_gu, b_gu, w_down, b_down, final_g, final, bm):
    n, d = x1.shape
    i32 = jnp.int32
    flat_e = topi[:, :TOP_K].reshape(-1)
    n_asn = n * TOP_K
    experts = jnp.arange(N_EXPERTS, dtype=i32)
    onehot = flat_e[:, None] == experts[None, :]
    counts = jnp.sum(onehot, axis=0, dtype=i32)
    padded = (counts + bm - 1) // bm * bm
    pend = jnp.cumsum(padded)
    pstart = pend - padded
    start = jnp.cumsum(counts) - counts
    order = jnp.argsort(flat_e).astype(i32)
    rank = jnp.argsort(order).astype(i32)
    shift = jnp.sum(jnp.where(onehot, (pstart - start)[None, :], 0), axis=1, dtype=i32)
    dest = rank + shift
    n_blocks = -(-n_asn // bm) + N_EXPERTS
    n_slots = n_blocks * bm
    slot = jnp.arange(n_slots, dtype=i32)
    slot_e = jnp.minimum(jnp.sum(slot[:, None] >= pend[None, :], axis=1, dtype=i32), N_EXPERTS - 1)
    slot_hot = slot_e[:, None] == experts[None, :]
    pick = lambda v: jnp.sum(jnp.where(slot_hot, v[None, :], 0), axis=1, dtype=i32)
    pos = slot - pick(pstart)
    src = jnp.clip(pick(start) + pos, 0, n_asn - 1)
    slot_tok = jnp.where(pos < pick(counts), order[src] // TOP_K, 0).astype(i32)
    blk = jnp.arange(n_blocks, dtype=i32)
    n_used = pend[-1] // bm
    blk_v = (blk < n_used).astype(i32)
    blk_x = jnp.minimum(blk, n_used - 1)
    blk_e = jnp.minimum(jnp.sum((blk_x * bm)[:, None] >= pend[None, :], axis=1, dtype=i32), N_EXPERTS - 1)
    xs = _sc_gather(h2, slot_tok)
    ys = _moe_call(xs, blk_e, blk_x, blk_v, layer, w_gu, b_gu, w_down, b_down, bm)
    yg = _sc_gather(ys, dest.reshape(n, TOP_K).T.reshape(-1)).reshape(TOP_K, n, d // 2)
    return _combine_call(x1, yg, gates, g2, rows_per_mod, final_g, final)


def _conv_kernel(prev_ref, cur_ref, next_ref, w_ref, bdw_ref, lg_ref, lb_ref, o_ref, buf_ref, cv_ref):
    i = pl.program_id(1)
    n_i = pl.num_programs(1)
    tr = cur_ref.shape[0]
    d = cur_ref.shape[1]
    hl = CONV_HALO
    pad = CONV_KERNEL // 2
    buf_ref[0:hl, :] = jnp.where(i > 0, prev_ref[...].astype(F32), 0.0)
    buf_ref[hl:hl + tr, :] = cur_ref[...].astype(F32)
    buf_ref[hl + tr:, :] = jnp.where(i < n_i - 1, next_ref[...].astype(F32), 0.0)
    s1 = jnp.zeros((tr, LANES), F32)
    for cb in range(d // LANES):
        ls = slice(cb * LANES, (cb + 1) * LANES)
        acc = jnp.zeros((tr, LANES), F32) + bdw_ref[:, ls]
        for k in range(CONV_KERNEL):
            off = hl - pad + k
            acc = acc + w_ref[k:k + 1, ls] * buf_ref[off:off + tr, ls]
        cv_ref[:, ls] = acc
        s1 = s1 + acc
    mu = jnp.sum(s1, axis=-1, keepdims=True) * (1.0 / d)
    xc = cv_ref[...] - mu
    var = jnp.mean(xc * xc, axis=-1, keepdims=True)
    y = xc * lax.rsqrt(var + NORM_EPS) * lg_ref[...] + lb_ref[...]
    o_ref[...] = (y * jax.nn.sigmoid(y)).astype(o_ref.dtype)


def _conv_call(a3, w_dw, b_dw, ln_g, ln_b, tr=256):
    b, t, d = a3.shape
    tr = min(tr, t)
    hl = CONV_HALO
    rb = tr // hl
    n_hb = t // hl
    kpad = -(-CONV_KERNEL // 8) * 8
    w = jnp.zeros((kpad, d), F32).at[:CONV_KERNEL].set(w_dw)
    vec = lambda v: v.reshape(1, d)
    full = lambda shape: pl.BlockSpec(shape, lambda bi, i: (0, 0))
    return pl.pallas_call(
        _conv_kernel,
        grid=(b, t // tr),
        in_specs=[pl.BlockSpec((None, hl, d), lambda bi, i: (bi, jnp.maximum(i * rb - 1, 0), 0)),
                  pl.BlockSpec((None, tr, d), lambda bi, i: (bi, i, 0)),
                  pl.BlockSpec((None, hl, d), lambda bi, i: (bi, jnp.minimum((i + 1) * rb, n_hb - 1), 0)),
                  full((kpad, d)), full((1, d)), full((1, d)), full((1, d))],
        out_specs=pl.BlockSpec((None, tr, d), lambda bi, i: (bi, i, 0)),
        out_shape=jax.ShapeDtypeStruct((b, t, d), BF16),
        scratch_shapes=[pltpu.VMEM((tr + 2 * hl, d), F32), pltpu.VMEM((tr, d), F32)],
        compiler_params=_cparams(("parallel", "parallel")),
        name="dwconv_ln_silu",
    )(a3, a3, a3, w, vec(b_dw), vec(ln_g), vec(ln_b))


def _forward(x, c, ctx, c_ctx, norm_mix_g, norm_ffn_g, ada_w, ada_b,
             ev_w_in, ev_w_out, ev_hgrn_lb_logits, ev_hgrn_norm_g, ev_dattn_lambda, ev_dattn_subln_g,
             od_w_pw1, od_b_pw1, od_w_dw, od_b_dw, od_ln_g, od_ln_b, od_w_pw2, od_b_pw2,
             moe_w_router, moe_b_router, moe_w_gu, moe_b_gu, moe_w_down, moe_b_down, final_norm_g,
             moe_bm=1024):
    b, t, d = x.shape
    cl = ctx.shape[1]
    depth = ada_w.shape[0]
    n_hh = (d // 2) // HEAD
    n_ah = (d // 2) // HEAD
    n = b * t

    n_cond = -(-(b + 1) // 8) * 8
    cs = jnp.zeros((n_cond, d), F32).at[:b].set(c).at[b].set(c_ctx)
    mod = _ada_call(cs, ada_w, ada_b)

    x2 = x.reshape(n, d)
    ctx2 = ctx.reshape(b * cl, d)
    for l in range(depth):
        j = l // 2
        parts = [mod[l, :, i * d:(i + 1) * d] for i in range(N_MOD)]
        sh1, sc1, g1, sh2, sc2, g2 = [p[:b].reshape(b, 1, d) for p in parts]
        csh1, csc1 = [p[b:b + 1].reshape(1, 1, d) for p in parts[:2]]
        gmix = norm_mix_g[l].reshape(1, d)
        gffn = norm_ffn_g[l].reshape(1, d)
        if l % 2 == 0:
            w_in = ev_w_in[j].astype(BF16)
            p = _proj_call(x2, gmix, sh1, sc1, t, w_in)
            pc = _proj_call(ctx2, gmix, csh1, csc1, b * cl, w_in)
            p3 = p.reshape(b, t, p.shape[1])
            pc3 = pc.reshape(b, cl, pc.shape[1])
            p_lb = jax.nn.softmax(ev_hgrn_lb_logits.astype(F32), axis=1)
            lb = jnp.cumsum(p_lb, axis=1)[:, j].reshape(2, n_hh, 1, HEAD)
            yh = _hgrn_call(p3, pc3, lb, ev_hgrn_norm_g[j].reshape(1, HEAD), n_hh)
            lam_init = 0.8 - 0.6 * math.exp(-0.3 * l)
            lp = ev_dattn_lambda[j].astype(F32)
            lam = (jnp.exp(jnp.sum(lp[0] * lp[1])) - jnp.exp(jnp.sum(lp[2] * lp[3])) + lam_init).reshape(1)
            ya = _attn_call(p3, pc3, lam, ev_dattn_subln_g[j].reshape(1, HEAD), n_hh, n_ah, 1.0 - lam_init)
            w_out = ev_w_out[j].astype(BF16)
            hw = n_hh * HEAD
            acts = [yh.reshape(n, hw), ya.reshape(n, n_ah * HEAD)]
            ws = [w_out[:hw], w_out[hw:]]
            bias = None
        else:
            a = _proj_call(x2, gmix, sh1, sc1, t, od_w_pw1[j].astype(BF16), glu_bias=od_b_pw1[j])
            a = _conv_call(a.reshape(b, t, d), od_w_dw[j], od_b_dw[j], od_ln_g[j], od_ln_b[j])
            acts = [a.reshape(n, d)]
            ws = [od_w_pw2[j].astype(BF16)]
            bias = od_b_pw2[j]
        x1, h2, topi, gates = _resid_router_call(acts, ws, bias, x2, g1, gffn, sh2, sc2, t,
                                                 moe_w_router[l], moe_b_router[l])
        x2 = _moe_layer(x1, h2, topi, gates, g2, t, l, moe_w_gu, moe_b_gu, moe_w_down, moe_b_down,
                        final_norm_g, l == depth - 1, moe_bm)
    return x2.reshape(b, t, d)


def kernel(x, c, ctx, c_ctx, norm_mix_g, norm_ffn_g, ada_w, ada_b, ev_w_in, ev_w_out, ev_hgrn_lb_logits, ev_hgrn_norm_g, ev_dattn_lambda, ev_dattn_subln_g, od_w_pw1, od_b_pw1, od_w_dw, od_b_dw, od_ln_g, od_ln_b, od_w_pw2, od_b_pw2, moe_w_router, moe_b_router, moe_w_gu, moe_b_gu, moe_w_down, moe_b_down, final_norm_g):
    return _forward(x, c, ctx, c_ctx, norm_mix_g, norm_ffn_g, ada_w, ada_b, ev_w_in, ev_w_out,
                    ev_hgrn_lb_logits, ev_hgrn_norm_g, ev_dattn_lambda, ev_dattn_subln_g,
                    od_w_pw1, od_b_pw1, od_w_dw, od_b_dw, od_ln_g, od_ln_b, od_w_pw2, od_b_pw2,
                    moe_w_router, moe_b_router, moe_w_gu, moe_b_gu, moe_w_down, moe_b_down, final_norm_g)
```

```python
import functools
import math

import numpy as np
import jax
import jax.numpy as jnp
from jax import lax
from jax.experimental import pallas as pl
from jax.experimental.pallas import tpu as pltpu
from jax.experimental.pallas import tpu_sc as plsc

F32 = jnp.float32
BF16 = jnp.bfloat16

NORM_EPS = 1e-6
GRID_W = 64
N_MOD = 6
HEAD = 128
DATTN_HEAD_DIM = 64
ROPE_BASE = 10000.0
CONV_KERNEL = 31
CONV_HALO = 16
N_EXPERTS = 32
TOP_K = 4
SWIGLU_ALPHA = 1.702
SWIGLU_LIMIT = 7.0
LANES = 128
SUBLANES = 8
HGRN_CHUNK = 128
ATTN_SUB_ROWS = 256
NEG_BIG = -1e30
SC_CORES = 2
SC_SUBCORES = 16
SC_GATHER_ROWS = 32
HI_MASK = 0xFFFF0000

VMEM_LIMIT = 56 * 1024 * 1024


def _cparams(sem):
    return pltpu.CompilerParams(dimension_semantics=sem, vmem_limit_bytes=VMEM_LIMIT)


def _dot(a, b):
    return jnp.dot(a, b, preferred_element_type=F32)


def _dot_nt(a, b):
    return lax.dot_general(a, b, (((1,), (1,)), ((), ())), preferred_element_type=F32)


def _dot_tn(a, b):
    return lax.dot_general(a, b, (((0,), (0,)), ((), ())), preferred_element_type=F32)


def _split3(x):
    hi = x.astype(BF16)
    r1 = x - hi.astype(F32)
    mid = r1.astype(BF16)
    lo = (r1 - mid.astype(F32)).astype(BF16)
    return hi, mid, lo


def _norm_mod(x, g, shift, scale):
    ms = jnp.mean(x * x, axis=-1, keepdims=True)
    return (x * lax.rsqrt(ms + NORM_EPS) * g) * (1.0 + scale) + shift


def _pack_halves(a):
    half = a.shape[1] // 2
    lo = pltpu.bitcast(a[:, :half], jnp.uint32) >> 16
    hi = pltpu.bitcast(a[:, half:], jnp.uint32) & jnp.uint32(HI_MASK)
    return hi | lo


def _unpack_halves(p):
    lo = pltpu.bitcast(p << 16, F32)
    hi = pltpu.bitcast(p & jnp.uint32(HI_MASK), F32)
    return lo, hi


def _sc_gather(table, idx):
    m = idx.shape[0]
    w = table.shape[1]
    r = SC_GATHER_ROWS
    n_workers = SC_CORES * SC_SUBCORES
    per_worker = m // n_workers
    n_steps = per_worker // r
    assert per_worker * n_workers == m and n_steps * r == per_worker and n_steps % 2 == 0 and n_steps >= 2
    mesh = plsc.VectorSubcoreMesh(core_axis_name="c", subcore_axis_name="s")

    @functools.partial(
        pl.kernel, mesh=mesh,
        out_type=jax.ShapeDtypeStruct((m, w), table.dtype),
        scratch_types=[pltpu.VMEM((per_worker,), jnp.int32), pltpu.VMEM((2, r, w), table.dtype),
                       pltpu.SemaphoreType.DMA((2,)), pltpu.SemaphoreType.DMA((2,))],
    )
    def gather_kernel(t_hbm, i_hbm, o_hbm, idx_v, rows_v, gsem, osem):
        base = (lax.axis_index("s") * SC_CORES + lax.axis_index("c")) * per_worker
        pltpu.sync_copy(i_hbm.at[pl.ds(base, per_worker)], idx_v)

        def fetch(j, b):
            return pltpu.make_async_copy(t_hbm.at[idx_v.at[pl.ds(j * r, r)]], rows_v.at[b], gsem.at[b])

        def flush(j, b):
            return pltpu.make_async_copy(rows_v.at[b], o_hbm.at[pl.ds(base + j * r, r)], osem.at[b])

        fetch(0, 0).start()

        @pl.loop(0, n_steps, step=2)
        def _(j0):
            for b in range(2):
                j = j0 + b
                fetch(j, b).wait()
                flush(j, b).start()

                @pl.when(j + 1 < n_steps)
                def _():
                    @pl.when(j >= 1)
                    def _():
                        flush(j - 1, 1 - b).wait()

                    fetch(j + 1, 1 - b).start()

        flush(n_steps - 2, 0).wait()
        flush(n_steps - 1, 1).wait()

    return gather_kernel(table, idx)


def _ada_kernel(cs_ref, w_ref, b_ref, o_ref):
    cs = cs_ref[...]
    s = cs * jax.nn.sigmoid(cs)
    hi, mid, lo = _split3(s)
    wh, wm, wl = _split3(w_ref[...])
    acc = _dot(hi, wh) + (_dot(hi, wm) + _dot(mid, wh)) + (_dot(hi, wl) + _dot(mid, wm) + _dot(lo, wh))
    o_ref[...] = acc + b_ref[...]


def _ada_call(cs, ada_w, ada_b):
    n_layers, d, n6 = ada_w.shape
    r = cs.shape[0]
    tn = min(512, n6)
    return pl.pallas_call(
        _ada_kernel,
        grid=(n_layers, n6 // tn),
        in_specs=[
            pl.BlockSpec((r, d), lambda l, j: (0, 0)),
            pl.BlockSpec((None, d, tn), lambda l, j: (l, 0, j)),
            pl.BlockSpec((None, 1, tn), lambda l, j: (l, 0, j)),
        ],
        out_specs=pl.BlockSpec((None, r, tn), lambda l, j: (l, 0, j)),
        out_shape=jax.ShapeDtypeStruct((n_layers, r, n6), F32),
        compiler_params=_cparams(("parallel", "parallel")),
        name="ada_mod",
    )(cs, ada_w, ada_b.reshape(n_layers, 1, n6))


def _proj_kernel(x_ref, g_ref, sh_ref, sc_ref, w_ref, o_ref, h_ref):
    @pl.when(pl.program_id(1) == 0)
    def _():
        h_ref[...] = _norm_mod(x_ref[...], g_ref[...], sh_ref[...], sc_ref[...]).astype(BF16)

    o_ref[...] = _dot(h_ref[...], w_ref[...]).astype(o_ref.dtype)


def _proj_glu_kernel(x_ref, g_ref, sh_ref, sc_ref, wa_ref, wb_ref, ba_ref, bb_ref, o_ref, h_ref):
    @pl.when(pl.program_id(1) == 0)
    def _():
        h_ref[...] = _norm_mod(x_ref[...], g_ref[...], sh_ref[...], sc_ref[...]).astype(BF16)

    h = h_ref[...]
    a = _dot(h, wa_ref[...]) + ba_ref[...]
    b = _dot(h, wb_ref[...]) + bb_ref[...]
    o_ref[...] = (a * jax.nn.sigmoid(b)).astype(o_ref.dtype)


def _proj_call(x2, g, shift, scale, rows_per_mod, w, glu_bias=None, tm=1024, tn=1024):
    m, d = x2.shape
    n_out = w.shape[1] if glu_bias is None else w.shape[1] // 2
    tm = min(tm, m, rows_per_mod)
    tn = min(tn, n_out)
    bpm = rows_per_mod // tm
    row_spec = pl.BlockSpec((tm, d), lambda i, j: (i, 0))
    vec_spec = pl.BlockSpec((1, d), lambda i, j: (0, 0))
    mod_spec = pl.BlockSpec((None, 1, d), lambda i, j: (i // bpm, 0, 0))
    out_spec = pl.BlockSpec((tm, tn), lambda i, j: (i, j))
    common = dict(
        grid=(m // tm, n_out // tn),
        out_specs=out_spec,
        out_shape=jax.ShapeDtypeStruct((m, n_out), BF16),
        scratch_shapes=[pltpu.VMEM((tm, d), BF16)],
        compiler_params=_cparams(("parallel", "arbitrary")),
    )
    if glu_bias is None:
        return pl.pallas_call(
            _proj_kernel,
            in_specs=[row_spec, vec_spec, mod_spec, mod_spec,
                      pl.BlockSpec((d, tn), lambda i, j: (0, j))],
            name="norm_proj",
            **common,
        )(x2, g, shift, scale, w)
    nb = n_out // tn
    bias = glu_bias.reshape(1, 2 * n_out)
    return pl.pallas_call(
        _proj_glu_kernel,
        in_specs=[row_spec, vec_spec, mod_spec, mod_spec,
                  pl.BlockSpec((d, tn), lambda i, j: (0, j)),
                  pl.BlockSpec((d, tn), lambda i, j: (0, j + nb)),
                  pl.BlockSpec((1, tn), lambda i, j: (0, j)),
                  pl.BlockSpec((1, tn), lambda i, j: (0, j + nb))],
        name="norm_proj_glu",
        **common,
    )(x2, g, shift, scale, w, w, bias, bias)


def _hgrn_tables(c):
    n_lvl = int(math.log2(c))
    t = np.arange(c)[:, None]
    s = np.arange(c)[None, :]
    gs = [(s <= t).astype(np.float32)]
    masks = [(s == t).astype(np.float32)]
    for lvl in range(1, n_lvl + 1):
        b = 2 ** (lvl - 1)
        mid = (t // (2 * b)) * (2 * b) + b - 1
        upper = t > mid
        g = np.where(upper & (s > mid) & (s <= t), 1.0, 0.0) - np.where((~upper) & (s > t) & (s <= mid), 1.0, 0.0)
        same = (t // (2 * b)) == (s // (2 * b))
        gs.append(g.astype(np.float32))
        masks.append((same & upper & (s <= mid)).astype(np.float32))
    gs_f = np.concatenate(gs, axis=0)
    mk_f = np.stack(masks, axis=0)
    gs_b = np.concatenate([g[::-1, ::-1] for g in gs], axis=0)
    mk_b = np.stack([m[::-1, ::-1] for m in masks], axis=0)
    return np.stack([gs_f, gs_b]), np.stack([mk_f, mk_b])


def _hgrn_gate(z, lb):
    e = jnp.exp(-jnp.abs(z))
    r = 1.0 / (1.0 + e)
    er = e * r
    pos = z >= 0
    sig = jnp.where(pos, r, er)
    sig_neg = jnp.where(pos, er, r)
    logf = jnp.log(lb + (1.0 - lb) * sig)
    return logf, (1.0 - lb) * sig_neg


def _gs_dot(gs, logf):
    hi, mid, lo = _split3(logf)
    return _dot(gs, hi) + _dot(gs, mid) + _dot(gs, lo)


def _hgrn_state_step(z, v, lb, tri, last_row, st):
    logf, k = _hgrn_gate(z, lb)
    cum = _gs_dot(tri, logf)
    tot = cum[last_row:last_row + 1, :]
    kt = (k * jnp.exp(tot - cum)).astype(BF16)
    return st * jnp.exp(tot) + _dot_tn(v, kt)


def _hgrn_chunk_step(q, z, v, lb, gs_ref, mk_ref, d, last_row, st):
    c = q.shape[0]
    n_lvl = mk_ref.shape[1] - 1
    logf, k = _hgrn_gate(z, lb)
    dist = _gs_dot(gs_ref[d], logf)
    cum = dist[0:c]
    tot = cum[last_row:last_row + 1, :]
    qf = q.astype(F32)
    att = mk_ref[d, 0] * _dot_nt(q, k.astype(BF16))
    for lvl in range(1, n_lvl + 1):
        e = jnp.exp(-jnp.abs(dist[lvl * c:(lvl + 1) * c]))
        att = att + mk_ref[d, lvl] * _dot_nt((qf * e).astype(BF16), (k * e).astype(BF16))
    o = _dot(att.astype(BF16), v) + _dot_nt((qf * jnp.exp(cum)).astype(BF16), st.astype(BF16))
    kt = (k * jnp.exp(tot - cum)).astype(BF16)
    st_new = st * jnp.exp(tot) + _dot_tn(v, kt)
    return o, st_new


def _hgrn_kernel(q_ref, v_ref, zf_ref, zb_ref, g_ref, vc_ref, zfc_ref, zbc_ref, lb_ref, gn_ref,
                 gs_ref, mk_ref, o_ref, acc_ref):
    c = HGRN_CHUNK
    t_len = q_ref.shape[0]
    n = t_len // c
    nc = vc_ref.shape[0] // c
    lbf = lb_ref[0]
    lbb = lb_ref[1]
    acc_ref[...] = jnp.zeros_like(acc_ref)
    st0 = jnp.zeros((HEAD, HEAD), F32)

    def rows(i):
        return pl.ds(pl.multiple_of(i * c, c), c)

    def ctx_body(i, carry):
        sf, sb = carry
        rf = rows(i)
        rb = rows(nc - 1 - i)
        sf = _hgrn_state_step(zfc_ref[rf, :].astype(F32), vc_ref[rf, :], lbf, gs_ref[0, 0:c, :], c - 1, sf)
        sb = _hgrn_state_step(zbc_ref[rb, :].astype(F32), vc_ref[rb, :], lbb, gs_ref[1, 0:c, :], 0, sb)
        return sf, sb

    sf, sb = lax.fori_loop(0, nc, ctx_body, (st0, st0))

    def body(i, carry):
        sf, sb = carry
        rf = rows(i)
        rb = rows(n - 1 - i)
        of, sf = _hgrn_chunk_step(q_ref[rf, :], zf_ref[rf, :].astype(F32), v_ref[rf, :], lbf,
                                  gs_ref, mk_ref, 0, c - 1, sf)
        acc_ref[rf, :] += of
        ob, sb = _hgrn_chunk_step(q_ref[rb, :], zb_ref[rb, :].astype(F32), v_ref[rb, :], lbb,
                                  gs_ref, mk_ref, 1, 0, sb)
        acc_ref[rb, :] += ob
        return sf, sb

    lax.fori_loop(0, n, body, (sf, sb))

    o = acc_ref[...]
    gate = g_ref[...].astype(F32)
    y = o * lax.rsqrt(jnp.mean(o * o, axis=-1, keepdims=True) + NORM_EPS) * gn_ref[...]
    o_ref[...] = (y * (gate * jax.nn.sigmoid(gate))).astype(o_ref.dtype)


def _hgrn_call(p3, pc3, lb, gn, n_heads):
    b, t, _ = p3.shape
    cl = pc3.shape[1]
    h = n_heads
    gs, mk = _hgrn_tables(HGRN_CHUNK)
    gs = jnp.asarray(gs, BF16)
    mk = jnp.asarray(mk, F32)

    def col(seg, rows_):
        return pl.BlockSpec((None, rows_, HEAD), lambda bi, hi, seg=seg: (bi, 0, seg * h + hi))

    return pl.pallas_call(
        _hgrn_kernel,
        grid=(b, h),
        in_specs=[col(0, t), col(1, t), col(2, t), col(3, t), col(4, t),
                  col(1, cl), col(2, cl), col(3, cl),
                  pl.BlockSpec((2, None, 1, HEAD), lambda bi, hi: (0, hi, 0, 0)),
                  pl.BlockSpec((1, HEAD), lambda bi, hi: (0, 0)),
                  pl.BlockSpec(gs.shape, lambda bi, hi: (0, 0, 0)),
                  pl.BlockSpec(mk.shape, lambda bi, hi: (0, 0, 0, 0))],
        out_specs=pl.BlockSpec((None, t, HEAD), lambda bi, hi: (bi, 0, hi)),
        out_shape=jax.ShapeDtypeStruct((b, t, h * HEAD), BF16),
        scratch_shapes=[pltpu.VMEM((t, HEAD), F32)],
        compiler_params=_cparams(("parallel", "parallel")),
        name="hgrn2",
    )(p3, p3, p3, p3, p3, pc3, pc3, pc3, lb, gn, gs, mk)


def _rope_tables(t_len):
    rows = t_len // GRID_W
    row = jnp.repeat(jnp.arange(rows, dtype=F32), GRID_W)
    colp = jnp.tile(jnp.arange(GRID_W, dtype=F32), rows)
    n_freq = DATTN_HEAD_DIM // 4
    inv = ROPE_BASE ** (-jnp.arange(n_freq, dtype=F32) / n_freq)
    ang_r = row[:, None] * inv
    ang_c = colp[:, None] * inv
    cos64 = jnp.concatenate([jnp.cos(ang_r), jnp.cos(ang_r), jnp.cos(ang_c), jnp.cos(ang_c)], axis=1)
    sin64 = jnp.concatenate([-jnp.sin(ang_r), jnp.sin(ang_r), -jnp.sin(ang_c), jnp.sin(ang_c)], axis=1)
    lane = np.arange(HEAD)
    partner = np.where(lane % (2 * n_freq) < n_freq, lane + n_freq, lane - n_freq)
    perm = np.zeros((HEAD, HEAD), np.float32)
    perm[partner, lane] = 1.0
    return jnp.tile(cos64, (1, 2)), jnp.tile(sin64, (1, 2)), jnp.asarray(perm, BF16)


def _attn_kernel(lam_ref, q_ref, k_ref, v_ref, kc_ref, vc_ref, cos_ref, sin_ref, perm_ref, sg_ref,
                 o_ref, kall_ref, vall_ref, *, out_scale):
    qi = pl.program_id(2)
    tq = q_ref.shape[0]
    cl = kc_ref.shape[0]

    @pl.when(qi == 0)
    def _():
        k = k_ref[...]
        kr = k.astype(F32) * cos_ref[...] + _dot(k, perm_ref[...]) * sin_ref[...]
        kall_ref[cl:, :] = kr.astype(BF16)
        kall_ref[0:cl, :] = kc_ref[...]
        vall_ref[cl:, 0:HEAD] = v_ref[...]
        vall_ref[0:cl, 0:HEAD] = vc_ref[...]
        vall_ref[:, HEAD:] = jnp.ones((vall_ref.shape[0], HEAD), BF16)

    r0 = pl.multiple_of(qi * tq, tq)
    q = q_ref[...]
    cq = cos_ref[pl.ds(r0, tq), :]
    sq = sin_ref[pl.ds(r0, tq), :]
    qr = (q.astype(F32) * cq + _dot(q, perm_ref[...]) * sq) * (DATTN_HEAD_DIM ** -0.5)
    lane = lax.broadcasted_iota(jnp.int32, (tq, HEAD), 1)
    q0 = jnp.where(lane < DATTN_HEAD_DIM, qr, 0.0).astype(BF16)
    q1 = jnp.where(lane >= DATTN_HEAD_DIM, qr, 0.0).astype(BF16)
    lam = lam_ref[0]

    def softmax_av(qm):
        s = _dot_nt(qm, kall_ref[...])
        m = jnp.max(s, axis=-1, keepdims=True)
        p = jnp.exp((s - m).astype(BF16))
        oa = _dot(p, vall_ref[...])
        return oa[:, 0:HEAD] / oa[:, HEAD:]

    sub = min(ATTN_SUB_ROWS, tq)
    for r in range(tq // sub):
        rs = slice(r * sub, (r + 1) * sub)
        o = softmax_av(q0[rs]) - lam * softmax_av(q1[rs])
        y = o * lax.rsqrt(jnp.mean(o * o, axis=-1, keepdims=True) + NORM_EPS) * sg_ref[...] * out_scale
        o_ref[rs, :] = y.astype(o_ref.dtype)


def _attn_call(p3, pc3, lam, subln_g, n_hgrn_heads, n_heads, out_scale, tq=512):
    b, t, _ = p3.shape
    cl = pc3.shape[1]
    tq = min(tq, t)
    base = 5 * n_hgrn_heads
    cos_t, sin_t, perm = _rope_tables(t)

    def col(seg, rows_):
        return pl.BlockSpec((None, rows_, HEAD), lambda bi, hi, qi, seg=seg: (bi, 0, base + seg * n_heads + hi))

    full2 = lambda shape: pl.BlockSpec(shape, lambda bi, hi, qi: (0, 0))
    return pl.pallas_call(
        functools.partial(_attn_kernel, out_scale=out_scale),
        grid=(b, n_heads, t // tq),
        in_specs=[pl.BlockSpec(memory_space=pltpu.SMEM),
                  pl.BlockSpec((None, tq, HEAD), lambda bi, hi, qi: (bi, qi, base + hi)),
                  col(1, t), col(2, t), col(1, cl), col(2, cl),
                  full2((t, HEAD)), full2((t, HEAD)), full2((HEAD, HEAD)), full2((1, HEAD))],
        out_specs=pl.BlockSpec((None, tq, HEAD), lambda bi, hi, qi: (bi, qi, hi)),
        out_shape=jax.ShapeDtypeStruct((b, t, n_heads * HEAD), BF16),
        scratch_shapes=[pltpu.VMEM((cl + t, HEAD), BF16), pltpu.VMEM((cl + t, 2 * HEAD), BF16)],
        compiler_params=_cparams(("parallel", "parallel", "arbitrary")),
        name="diff_attn",
    )(lam, p3, p3, p3, pc3, pc3, cos_t, sin_t, perm, subln_g)


def _resid_router_kernel(*refs, n_act, has_bias):
    acts = refs[0:n_act]
    ws = refs[n_act:2 * n_act]
    pos = 2 * n_act
    bias_ref = refs[pos] if has_bias else None
    pos += int(has_bias)
    x_ref, g1_ref, gf_ref, sh_ref, sc_ref, wrh_ref, wrl_ref, br_ref = refs[pos:pos + 8]
    x1_ref, h2_ref, ti_ref, gt_ref = refs[pos + 8:pos + 12]

    y = _dot(acts[0][...], ws[0][...])
    for a, w in zip(acts[1:], ws[1:]):
        y = y + _dot(a[...], w[...])
    if has_bias:
        y = y + bias_ref[...]
    x1 = x_ref[...] + g1_ref[...] * y
    x1_ref[...] = x1
    h2 = _norm_mod(x1, gf_ref[...], sh_ref[...], sc_ref[...])
    hi = h2.astype(BF16)
    hi_f = hi.astype(F32)
    h2_ref[...] = _pack_halves(hi_f)
    lo = (h2 - hi_f).astype(BF16)
    wrh = wrh_ref[...]
    logits = _dot(hi, wrh) + (_dot(lo, wrh) + _dot(hi, wrl_ref[...])) + br_ref[...]

    tm = logits.shape[0]
    lane = lax.broadcasted_iota(jnp.int32, (tm, LANES), 1).astype(F32)
    vals, idxs = [], []
    rem = logits
    for _ in range(TOP_K):
        m = jnp.max(rem, axis=-1, keepdims=True)
        idx = jnp.min(jnp.where(rem == m, lane, float(LANES)), axis=-1, keepdims=True)
        vals.append(m)
        idxs.append(idx)
        rem = jnp.where(lane == idx, -jnp.inf, rem)
    es = [jnp.exp(v - vals[0]) for v in vals]
    inv = 1.0 / (es[0] + es[1] + es[2] + es[3])
    ti = jnp.zeros((tm, LANES), F32)
    gt = jnp.zeros((tm, LANES), F32)
    for k in range(TOP_K):
        ti = jnp.where(lane == float(k), idxs[k], ti)
        gt = jnp.where(lane == float(k), es[k] * inv, gt)
    ti_ref[...] = ti.astype(jnp.int32)
    gt_ref[...] = gt


def _resid_router_call(acts, ws, bias, x2, g1, gf, sh2, sc2, rows_per_mod, w_router, b_router, tm=256):
    m, d = x2.shape
    tm = min(tm, m, rows_per_mod)
    bpm = rows_per_mod // tm
    n_act = len(acts)
    wr = jnp.zeros((d, LANES), F32).at[:, :N_EXPERTS].set(w_router)
    wr_hi = wr.astype(BF16)
    wr_lo = (wr - wr_hi.astype(F32)).astype(BF16)
    br = jnp.full((1, LANES), NEG_BIG, F32).at[0, :N_EXPERTS].set(b_router)
    row = lambda width: pl.BlockSpec((tm, width), lambda i: (i, 0))
    full = lambda shape: pl.BlockSpec(shape, lambda i: (0, 0))
    mod_spec = pl.BlockSpec((None, 1, d), lambda i: (i // bpm, 0, 0))
    in_specs = [row(a.shape[1]) for a in acts] + [full(w.shape) for w in ws]
    args = list(acts) + list(ws)
    if bias is not None:
        in_specs.append(full((1, d)))
        args.append(bias.reshape(1, d))
    in_specs += [row(d), mod_spec, full((1, d)), mod_spec, mod_spec,
                 full((d, LANES)), full((d, LANES)), full((1, LANES))]
    args += [x2, g1, gf, sh2, sc2, wr_hi, wr_lo, br]
    return pl.pallas_call(
        functools.partial(_resid_router_kernel, n_act=n_act, has_bias=bias is not None),
        grid=(m // tm,),
        in_specs=in_specs,
        out_specs=[row(d), row(d // 2), row(LANES), row(LANES)],
        out_shape=[jax.ShapeDtypeStruct((m, d), F32), jax.ShapeDtypeStruct((m, d // 2), jnp.uint32),
                   jax.ShapeDtypeStruct((m, LANES), jnp.int32), jax.ShapeDtypeStruct((m, LANES), F32)],
        compiler_params=_cparams(("parallel",)),
        name="resid_router",
    )(*args)


def _moe_kernel(be_ref, bx_ref, bv_ref, x_ref, wg_ref, wu_ref, bg_ref, bu_ref, wd_ref, bd_ref, o_ref,
                xb_ref, acc_ref):
    i = pl.program_id(0)
    f = pl.program_id(1)
    last = pl.num_programs(1) - 1
    valid = bv_ref[i] > 0
    half = x_ref.shape[1]

    @pl.when(jnp.logical_and(valid, f == 0))
    def _():
        lo, hi = _unpack_halves(x_ref[...])
        xb_ref[:, 0:half] = lo.astype(BF16)
        xb_ref[:, half:] = hi.astype(BF16)
        acc_ref[...] = jnp.broadcast_to(bd_ref[...], acc_ref.shape)

    @pl.when(valid)
    def _():
        x = xb_ref[...]
        g = _dot(x, wg_ref[...].astype(BF16)) + bg_ref[...]
        u = _dot(x, wu_ref[...].astype(BF16)) + bu_ref[...]
        g = jnp.minimum(g, SWIGLU_LIMIT)
        u = jnp.clip(u, -SWIGLU_LIMIT, SWIGLU_LIMIT)
        a = g * jax.nn.sigmoid(SWIGLU_ALPHA * g) * (u + 1.0)
        acc_ref[...] += _dot(a.astype(BF16), wd_ref[...].astype(BF16))

    @pl.when(jnp.logical_and(valid, f == last))
    def _():
        o_ref[...] = _pack_halves(acc_ref[...].astype(BF16).astype(F32))

    @pl.when(jnp.logical_and(jnp.logical_not(valid), f == last))
    def _():
        o_ref[...] = jnp.zeros_like(o_ref)


def _moe_call(xs, blk_e, blk_x, blk_v, layer, w_gu, b_gu, w_down, b_down, bm, tf=256):
    n_slots, half = xs.shape
    d = 2 * half
    n_l, n_e, ff = w_down.shape[0:3]
    tf = min(tf, ff)
    nf = ff // tf
    nb = n_slots // bm
    fsel = lambda f, bv, i: jnp.where(bv[i] > 0, f, nf - 1)
    grid_spec = pltpu.PrefetchScalarGridSpec(
        num_scalar_prefetch=3,
        grid=(nb, nf),
        in_specs=[
            pl.BlockSpec((bm, half), lambda i, f, be, bx, bv: (bx[i], 0)),
            pl.BlockSpec((None, None, d, tf), lambda i, f, be, bx, bv: (layer, be[i], 0, fsel(f, bv, i))),
            pl.BlockSpec((None, None, d, tf), lambda i, f, be, bx, bv: (layer, be[i], 0, fsel(f, bv, i) + nf)),
            pl.BlockSpec((None, None, 1, tf), lambda i, f, be, bx, bv: (layer, be[i], 0, fsel(f, bv, i))),
            pl.BlockSpec((None, None, 1, tf), lambda i, f, be, bx, bv: (layer, be[i], 0, fsel(f, bv, i) + nf)),
            pl.BlockSpec((None, None, tf, d), lambda i, f, be, bx, bv: (layer, be[i], fsel(f, bv, i), 0)),
            pl.BlockSpec((None, None, 1, d), lambda i, f, be, bx, bv: (layer, be[i], 0, 0)),
        ],
        out_specs=pl.BlockSpec((bm, half), lambda i, f, be, bx, bv: (i, 0)),
        scratch_shapes=[pltpu.VMEM((bm, d), BF16), pltpu.VMEM((bm, d), F32)],
    )
    b_gu4 = b_gu.reshape(n_l, n_e, 1, 2 * ff)
    return pl.pallas_call(
        _moe_kernel,
        grid_spec=grid_spec,
        out_shape=jax.ShapeDtypeStruct((n_slots, half), jnp.uint32),
        compiler_params=_cparams(("arbitrary", "arbitrary")),
        name="moe_experts",
    )(blk_e, blk_x, blk_v, xs, w_gu, w_gu, b_gu4, b_gu4, w_down, b_down.reshape(n_l, n_e, 1, d))


def _combine_kernel(x_ref, yg_ref, gt_ref, g2_ref, fg_ref, o_ref, *, final):
    gt = gt_ref[...]
    acc_lo, acc_hi = None, None
    for k in range(TOP_K):
        lo, hi = _unpack_halves(yg_ref[k])
        gk = gt[:, k:k + 1]
        acc_lo = lo * gk if acc_lo is None else acc_lo + lo * gk
        acc_hi = hi * gk if acc_hi is None else acc_hi + hi * gk
    acc = jnp.concatenate([acc_lo, acc_hi], axis=1)
    x2 = x_ref[...] + g2_ref[...] * acc
    if final:
        x2 = x2 * lax.rsqrt(jnp.mean(x2 * x2, axis=-1, keepdims=True) + NORM_EPS) * fg_ref[...]
    o_ref[...] = x2


def _combine_call(x1, yg, gates, g2, rows_per_mod, final_g, final, tm=256):
    m, d = x1.shape
    tm = min(tm, m, rows_per_mod)
    bpm = rows_per_mod // tm
    return pl.pallas_call(
        functools.partial(_combine_kernel, final=final),
        grid=(m // tm,),
        in_specs=[pl.BlockSpec((tm, d), lambda i: (i, 0)),
                  pl.BlockSpec((TOP_K, tm, d // 2), lambda i: (0, i, 0)),
                  pl.BlockSpec((tm, LANES), lambda i: (i, 0)),
                  pl.BlockSpec((None, 1, d), lambda i: (i // bpm, 0, 0)),
                  pl.BlockSpec((1, d), lambda i: (0, 0))],
        out_specs=pl.BlockSpec((tm, d), lambda i: (i, 0)),
        out_shape=jax.ShapeDtypeStruct((m, d), F32),
        compiler_params=_cparams(("parallel",)),
        name="moe_combine",
    )(x1, yg, gates, g2, final_g.reshape(1, d))


def _moe_layer(x1, h2, topi, gates, g2, rows_per_mod, layer, w_gu, b_gu, w_down, b_down, final_g, final, bm):
    n, d = x1.shape
    i32 = jnp.int32
    flat_e = topi[:, :TOP_K].reshape(-1)
    n_asn = n * TOP_K
    experts = jnp.arange(N_EXPERTS, dtype=i32)
    onehot = flat_e[:, None] == experts[None, :]
    counts = jnp.sum(onehot, axis=0, dtype=i32)
    padded = (counts + bm - 1) // bm * bm
    pend = jnp.cumsum(padded)
    pstart = pend - padded
    start = jnp.cumsum(counts) - counts
    order = jnp.argsort(flat_e).astype(i32)
    rank = jnp.argsort(order).astype(i32)
    shift = jnp.sum(jnp.where(onehot, (pstart - start)[None, :], 0), axis=1, dtype=i32)
    dest = rank + shift
    n_blocks = -(-n_asn // bm) + N_EXPERTS
    n_slots = n_blocks * bm
    slot = jnp.arange(n_slots, dtype=i32)
    slot_e = jnp.minimum(jnp.sum(slot[:, None] >= pend[None, :], axis=1, dtype=i32), N_EXPERTS - 1)
    slot_hot = slot_e[:, None] == experts[None, :]
    pick = lambda v: jnp.sum(jnp.where(slot_hot, v[None, :], 0), axis=1, dtype=i32)
    pos = slot - pick(pstart)
    src = jnp.clip(pick(start) + pos, 0, n_asn - 1)
    slot_tok = jnp.where(pos < pick(counts), order[src] // TOP_K, slot % n).astype(i32)
    blk = jnp.arange(n_blocks, dtype=i32)
    n_used = pend[-1] // bm
    blk_v = (blk < n_used).astype(i32)
    blk_x = jnp.minimum(blk, n_used - 1)
    blk_e = jnp.minimum(jnp.sum((blk_x * bm)[:, None] >= pend[None, :], axis=1, dtype=i32), N_EXPERTS - 1)
    xs = _sc_gather(h2, slot_tok)
    ys = _moe_call(xs, blk_e, blk_x, blk_v, layer, w_gu, b_gu, w_down, b_down, bm)
    yg = _sc_gather(ys, dest.reshape(n, TOP_K).T.reshape(-1)).reshape(TOP_K, n, d // 2)
    return _combine_call(x1, yg, gates, g2, rows_per_mod, final_g, final)


def _conv_kernel(prev_ref, cur_ref, next_ref, w_ref, bdw_ref, lg_ref, lb_ref, o_ref, buf_ref, sh_ref, cv_ref):
    i = pl.program_id(1)
    n_i = pl.num_programs(1)
    tr = cur_ref.shape[0]
    d = cur_ref.shape[1]
    hl = CONV_HALO
    pad = CONV_KERNEL // 2
    sub = SUBLANES
    rows_sh = sh_ref.shape[1]
    buf_ref[0:hl, :] = jnp.where(i > 0, prev_ref[...].astype(F32), 0.0)
    buf_ref[hl:hl + tr, :] = cur_ref[...].astype(F32)
    buf_ref[hl + tr:, :] = jnp.where(i < n_i - 1, next_ref[...].astype(F32), 0.0)
    for s in range(1, sub):
        sh_ref[s - 1] = buf_ref[s:s + rows_sh, :]
    s1 = jnp.zeros((tr, LANES), F32)
    for cb in range(d // LANES):
        ls = slice(cb * LANES, (cb + 1) * LANES)
        acc = jnp.zeros((tr, LANES), F32) + bdw_ref[:, ls]
        for k in range(CONV_KERNEL):
            off = hl - pad + k
            base = off // sub * sub
            if off % sub == 0:
                rows = buf_ref[base:base + tr, ls]
            else:
                rows = sh_ref[off % sub - 1, base:base + tr, ls]
            acc = acc + w_ref[k:k + 1, ls] * rows
        cv_ref[:, ls] = acc
        s1 = s1 + acc
    mu = jnp.sum(s1, axis=-1, keepdims=True) * (1.0 / d)
    xc = cv_ref[...] - mu
    var = jnp.mean(xc * xc, axis=-1, keepdims=True)
    y = xc * lax.rsqrt(var + NORM_EPS) * lg_ref[...] + lb_ref[...]
    o_ref[...] = (y * jax.nn.sigmoid(y)).astype(o_ref.dtype)


def _conv_call(a3, w_dw, b_dw, ln_g, ln_b, tr=256):
    b, t, d = a3.shape
    tr = min(tr, t)
    hl = CONV_HALO
    rb = tr // hl
    n_hb = t // hl
    kpad = -(-CONV_KERNEL // 8) * 8
    w = jnp.zeros((kpad, d), F32).at[:CONV_KERNEL].set(w_dw)
    vec = lambda v: v.reshape(1, d)
    full = lambda shape: pl.BlockSpec(shape, lambda bi, i: (0, 0))
    return pl.pallas_call(
        _conv_kernel,
        grid=(b, t // tr),
        in_specs=[pl.BlockSpec((None, hl, d), lambda bi, i: (bi, jnp.maximum(i * rb - 1, 0), 0)),
                  pl.BlockSpec((None, tr, d), lambda bi, i: (bi, i, 0)),
                  pl.BlockSpec((None, hl, d), lambda bi, i: (bi, jnp.minimum((i + 1) * rb, n_hb - 1), 0)),
                  full((kpad, d)), full((1, d)), full((1, d)), full((1, d))],
        out_specs=pl.BlockSpec((None, tr, d), lambda bi, i: (bi, i, 0)),
        out_shape=jax.ShapeDtypeStruct((b, t, d), BF16),
        scratch_shapes=[pltpu.VMEM((tr + 2 * hl, d), F32),
                        pltpu.VMEM((SUBLANES - 1, tr + 2 * hl - SUBLANES, d), F32),
                        pltpu.VMEM((tr, d), F32)],
        compiler_params=_cparams(("parallel", "parallel")),
        name="dwconv_ln_silu",
    )(a3, a3, a3, w, vec(b_dw), vec(ln_g), vec(ln_b))


def _forward(x, c, ctx, c_ctx, norm_mix_g, norm_ffn_g, ada_w, ada_b,
             ev_w_in, ev_w_out, ev_hgrn_lb_logits, ev_hgrn_norm_g, ev_dattn_lambda, ev_dattn_subln_g,
             od_w_pw1, od_b_pw1, od_w_dw, od_b_dw, od_ln_g, od_ln_b, od_w_pw2, od_b_pw2,
             moe_w_router, moe_b_router, moe_w_gu, moe_b_gu, moe_w_down, moe_b_down, final_norm_g,
             moe_bm=1024):
    b, t, d = x.shape
    cl = ctx.shape[1]
    depth = ada_w.shape[0]
    n_hh = (d // 2) // HEAD
    n_ah = (d // 2) // HEAD
    n = b * t

    n_cond = -(-(b + 1) // 8) * 8
    cs = jnp.zeros((n_cond, d), F32).at[:b].set(c).at[b].set(c_ctx)
    mod = _ada_call(cs, ada_w, ada_b)

    x2 = x.reshape(n, d)
    ctx2 = ctx.reshape(b * cl, d)
    for l in range(depth):
        j = l // 2
        parts = [mod[l, :, i * d:(i + 1) * d] for i in range(N_MOD)]
        sh1, sc1, g1, sh2, sc2, g2 = [p[:b].reshape(b, 1, d) for p in parts]
        csh1, csc1 = [p[b:b + 1].reshape(1, 1, d) for p in parts[:2]]
        gmix = norm_mix_g[l].reshape(1, d)
        gffn = norm_ffn_g[l].reshape(1, d)
        if l % 2 == 0:
            w_in = ev_w_in[j].astype(BF16)
            p = _proj_call(x2, gmix, sh1, sc1, t, w_in)
            pc = _proj_call(ctx2, gmix, csh1, csc1, b * cl, w_in)
            p3 = p.reshape(b, t, p.shape[1])
            pc3 = pc.reshape(b, cl, pc.shape[1])
            p_lb = jax.nn.softmax(ev_hgrn_lb_logits.astype(F32), axis=1)
            lb = jnp.cumsum(p_lb, axis=1)[:, j].reshape(2, n_hh, 1, HEAD)
            yh = _hgrn_call(p3, pc3, lb, ev_hgrn_norm_g[j].reshape(1, HEAD), n_hh)
            lam_init = 0.8 - 0.6 * math.exp(-0.3 * l)
            lp = ev_dattn_lambda[j].astype(F32)
            lam = (jnp.exp(jnp.sum(lp[0] * lp[1])) - jnp.exp(jnp.sum(lp[2] * lp[3])) + lam_init).reshape(1)
            ya = _attn_call(p3, pc3, lam, ev_dattn_subln_g[j].reshape(1, HEAD), n_hh, n_ah, 1.0 - lam_init)
            w_out = ev_w_out[j].astype(BF16)
            hw = n_hh * HEAD
            acts = [yh.reshape(n, hw), ya.reshape(n, n_ah * HEAD)]
            ws = [w_out[:hw], w_out[hw:]]
            bias = None
        else:
            a = _proj_call(x2, gmix, sh1, sc1, t, od_w_pw1[j].astype(BF16), glu_bias=od_b_pw1[j])
            a = _conv_call(a.reshape(b, t, d), od_w_dw[j], od_b_dw[j], od_ln_g[j], od_ln_b[j])
            acts = [a.reshape(n, d)]
            ws = [od_w_pw2[j].astype(BF16)]
            bias = od_b_pw2[j]
        x1, h2, topi, gates = _resid_router_call(acts, ws, bias, x2, g1, gffn, sh2, sc2, t,
                                                 moe_w_router[l], moe_b_router[l])
        x2 = _moe_layer(x1, h2, topi, gates, g2, t, l, moe_w_gu, moe_b_gu, moe_w_down, moe_b_down,
                        final_norm_g, l == depth - 1, moe_bm)
    return x2.reshape(b, t, d)


def kernel(x, c, ctx, c_ctx, norm_mix_g, norm_ffn_g, ada_w, ada_b, ev_w_in, ev_w_out, ev_hgrn_lb_logits, ev_hgrn_norm_g, ev_dattn_lambda, ev_dattn_subln_g, od_w_pw1, od_b_pw1, od_w_dw, od_b_dw, od_ln_g, od_ln_b, od_w_pw2, od_b_pw2, moe_w_router, moe_b_router, moe_w_gu, moe_b_gu, moe_w_down, moe_b_down, final_norm_g):
    return _forward(x, c, ctx, c_ctx, norm_mix_g, norm_ffn_g, ada_w, ada_b, ev_w_in, ev_w_out,
                    ev_hgrn_lb_logits, ev_hgrn_norm_g, ev_dattn_lambda, ev_dattn_subln_g,
                    od_w_pw1, od_b_pw1, od_w_dw, od_b_dw, od_ln_g, od_ln_b, od_w_pw2, od_b_pw2,
                    moe_w_router, moe_b_router, moe_w_gu, moe_b_gu, moe_w_down, moe_b_down, final_norm_g)
```

```python
import functools
import math

import numpy as np
import jax
import jax.numpy as jnp
from jax import lax
from jax.experimental import pallas as pl
from jax.experimental.pallas import tpu as pltpu
from jax.experimental.pallas import tpu_sc as plsc

F32 = jnp.float32
BF16 = jnp.bfloat16

NORM_EPS = 1e-6
GRID_W = 64
N_MOD = 6
HEAD = 128
DATTN_HEAD_DIM = 64
ROPE_BASE = 10000.0
CONV_KERNEL = 31
CONV_HALO = 16
N_EXPERTS = 32
TOP_K = 4
SWIGLU_ALPHA = 1.702
SWIGLU_LIMIT = 7.0
LANES = 128
SUBLANES = 8
HGRN_CHUNK = 128
ATTN_SUB_ROWS = 256
NEG_BIG = -1e30
SC_CORES = 2
SC_SUBCORES = 16
SC_GATHER_ROWS = 32
HI_MASK = 0xFFFF0000

VMEM_LIMIT = 56 * 1024 * 1024


def _cparams(sem):
    return pltpu.CompilerParams(dimension_semantics=sem, vmem_limit_bytes=VMEM_LIMIT)


def _dot(a, b):
    return jnp.dot(a, b, preferred_element_type=F32)


def _dot_nt(a, b):
    return lax.dot_general(a, b, (((1,), (1,)), ((), ())), preferred_element_type=F32)


def _dot_tn(a, b):
    return lax.dot_general(a, b, (((0,), (0,)), ((), ())), preferred_element_type=F32)


def _split3(x):
    hi = x.astype(BF16)
    r1 = x - hi.astype(F32)
    mid = r1.astype(BF16)
    lo = (r1 - mid.astype(F32)).astype(BF16)
    return hi, mid, lo


def _norm_mod(x, g, shift, scale):
    ms = jnp.mean(x * x, axis=-1, keepdims=True)
    return (x * lax.rsqrt(ms + NORM_EPS) * g) * (1.0 + scale) + shift


def _pack_halves(a):
    half = a.shape[1] // 2
    lo = pltpu.bitcast(a[:, :half], jnp.uint32) >> 16
    hi = pltpu.bitcast(a[:, half:], jnp.uint32) & jnp.uint32(HI_MASK)
    return hi | lo


def _unpack_halves(p):
    lo = pltpu.bitcast(p << 16, F32)
    hi = pltpu.bitcast(p & jnp.uint32(HI_MASK), F32)
    return lo, hi


def _sc_gather(table, idx):
    m = idx.shape[0]
    w = table.shape[1]
    r = SC_GATHER_ROWS
    n_workers = SC_CORES * SC_SUBCORES
    per_worker = m // n_workers
    n_steps = per_worker // r
    assert per_worker * n_workers == m and n_steps * r == per_worker and n_steps % 2 == 0 and n_steps >= 2
    mesh = plsc.VectorSubcoreMesh(core_axis_name="c", subcore_axis_name="s")

    @functools.partial(
        pl.kernel, mesh=mesh,
        out_type=jax.ShapeDtypeStruct((m, w), table.dtype),
        scratch_types=[pltpu.VMEM((per_worker,), jnp.int32), pltpu.VMEM((2, r, w), table.dtype),
                       pltpu.SemaphoreType.DMA((2,)), pltpu.SemaphoreType.DMA((2,))],
    )
    def gather_kernel(t_hbm, i_hbm, o_hbm, idx_v, rows_v, gsem, osem):
        base = (lax.axis_index("s") * SC_CORES + lax.axis_index("c")) * per_worker
        pltpu.sync_copy(i_hbm.at[pl.ds(base, per_worker)], idx_v)

        def fetch(j, b):
            return pltpu.make_async_copy(t_hbm.at[idx_v.at[pl.ds(j * r, r)]], rows_v.at[b], gsem.at[b])

        def flush(j, b):
            return pltpu.make_async_copy(rows_v.at[b], o_hbm.at[pl.ds(base + j * r, r)], osem.at[b])

        fetch(0, 0).start()

        @pl.loop(0, n_steps, step=2)
        def _(j0):
            for b in range(2):
                j = j0 + b
                fetch(j, b).wait()
                flush(j, b).start()

                @pl.when(j + 1 < n_steps)
                def _():
                    @pl.when(j >= 1)
                    def _():
                        flush(j - 1, 1 - b).wait()

                    fetch(j + 1, 1 - b).start()

        flush(n_steps - 2, 0).wait()
        flush(n_steps - 1, 1).wait()

    return gather_kernel(table, idx)


def _ada_kernel(cs_ref, w_ref, b_ref, o_ref):
    cs = cs_ref[...]
    s = cs * jax.nn.sigmoid(cs)
    hi, mid, lo = _split3(s)
    wh, wm, wl = _split3(w_ref[...])
    acc = _dot(hi, wh) + (_dot(hi, wm) + _dot(mid, wh)) + (_dot(hi, wl) + _dot(mid, wm) + _dot(lo, wh))
    o_ref[...] = acc + b_ref[...]


def _ada_call(cs, ada_w, ada_b):
    n_layers, d, n6 = ada_w.shape
    r = cs.shape[0]
    tn = min(512, n6)
    return pl.pallas_call(
        _ada_kernel,
        grid=(n_layers, n6 // tn),
        in_specs=[
            pl.BlockSpec((r, d), lambda l, j: (0, 0)),
            pl.BlockSpec((None, d, tn), lambda l, j: (l, 0, j)),
            pl.BlockSpec((None, 1, tn), lambda l, j: (l, 0, j)),
        ],
        out_specs=pl.BlockSpec((None, r, tn), lambda l, j: (l, 0, j)),
        out_shape=jax.ShapeDtypeStruct((n_layers, r, n6), F32),
        compiler_params=_cparams(("parallel", "parallel")),
        name="ada_mod",
    )(cs, ada_w, ada_b.reshape(n_layers, 1, n6))


def _proj_kernel(x_ref, g_ref, sh_ref, sc_ref, w_ref, o_ref, h_ref):
    @pl.when(pl.program_id(1) == 0)
    def _():
        h_ref[...] = _norm_mod(x_ref[...], g_ref[...], sh_ref[...], sc_ref[...]).astype(BF16)

    o_ref[...] = _dot(h_ref[...], w_ref[...]).astype(o_ref.dtype)


def _proj_glu_kernel(x_ref, g_ref, sh_ref, sc_ref, wa_ref, wb_ref, ba_ref, bb_ref, o_ref, h_ref):
    @pl.when(pl.program_id(1) == 0)
    def _():
        h_ref[...] = _norm_mod(x_ref[...], g_ref[...], sh_ref[...], sc_ref[...]).astype(BF16)

    h = h_ref[...]
    a = _dot(h, wa_ref[...]) + ba_ref[...]
    b = _dot(h, wb_ref[...]) + bb_ref[...]
    o_ref[...] = (a * jax.nn.sigmoid(b)).astype(o_ref.dtype)


def _proj_call(x2, g, shift, scale, rows_per_mod, w, glu_bias=None, tm=1024, tn=1024):
    m, d = x2.shape
    n_out = w.shape[1] if glu_bias is None else w.shape[1] // 2
    tm = min(tm, m, rows_per_mod)
    tn = min(tn, n_out)
    bpm = rows_per_mod // tm
    row_spec = pl.BlockSpec((tm, d), lambda i, j: (i, 0))
    vec_spec = pl.BlockSpec((1, d), lambda i, j: (0, 0))
    mod_spec = pl.BlockSpec((None, 1, d), lambda i, j: (i // bpm, 0, 0))
    out_spec = pl.BlockSpec((tm, tn), lambda i, j: (i, j))
    common = dict(
        grid=(m // tm, n_out // tn),
        out_specs=out_spec,
        out_shape=jax.ShapeDtypeStruct((m, n_out), BF16),
        scratch_shapes=[pltpu.VMEM((tm, d), BF16)],
        compiler_params=_cparams(("parallel", "arbitrary")),
    )
    if glu_bias is None:
        return pl.pallas_call(
            _proj_kernel,
            in_specs=[row_spec, vec_spec, mod_spec, mod_spec,
                      pl.BlockSpec((d, tn), lambda i, j: (0, j))],
            name="norm_proj",
            **common,
        )(x2, g, shift, scale, w)
    nb = n_out // tn
    bias = glu_bias.reshape(1, 2 * n_out)
    return pl.pallas_call(
        _proj_glu_kernel,
        in_specs=[row_spec, vec_spec, mod_spec, mod_spec,
                  pl.BlockSpec((d, tn), lambda i, j: (0, j)),
                  pl.BlockSpec((d, tn), lambda i, j: (0, j + nb)),
                  pl.BlockSpec((1, tn), lambda i, j: (0, j)),
                  pl.BlockSpec((1, tn), lambda i, j: (0, j + nb))],
        name="norm_proj_glu",
        **common,
    )(x2, g, shift, scale, w, w, bias, bias)


def _hgrn_tables(c):
    n_lvl = int(math.log2(c))
    t = np.arange(c)[:, None]
    s = np.arange(c)[None, :]
    gs = [(s <= t).astype(np.float32)]
    masks = [(s == t).astype(np.float32)]
    for lvl in range(1, n_lvl + 1):
        b = 2 ** (lvl - 1)
        mid = (t // (2 * b)) * (2 * b) + b - 1
        upper = t > mid
        g = np.where(upper & (s > mid) & (s <= t), 1.0, 0.0) - np.where((~upper) & (s > t) & (s <= mid), 1.0, 0.0)
        same = (t // (2 * b)) == (s // (2 * b))
        gs.append(g.astype(np.float32))
        masks.append((same & upper & (s <= mid)).astype(np.float32))
    gs_f = np.concatenate(gs, axis=0)
    mk_f = np.stack(masks, axis=0)
    gs_b = np.concatenate([g[::-1, ::-1] for g in gs], axis=0)
    mk_b = np.stack([m[::-1, ::-1] for m in masks], axis=0)
    return np.stack([gs_f, gs_b]), np.stack([mk_f, mk_b])


def _hgrn_gate(z, lb):
    e = jnp.exp(-jnp.abs(z))
    r = 1.0 / (1.0 + e)
    er = e * r
    pos = z >= 0
    sig = jnp.where(pos, r, er)
    sig_neg = jnp.where(pos, er, r)
    logf = jnp.log(lb + (1.0 - lb) * sig)
    return logf, (1.0 - lb) * sig_neg


def _split2(x):
    hi = x.astype(BF16)
    return hi, (x - hi.astype(F32)).astype(BF16)


def _hgrn_state_step(z, v, lb, tri, last_row, st):
    logf, k = _hgrn_gate(z, lb)
    hi, mid = _split2(logf)
    cum = _dot(tri, hi) + _dot(tri, mid)
    tot = cum[last_row:last_row + 1, :]
    kt = (k * jnp.exp(tot - cum)).astype(BF16)
    return st * jnp.exp(tot) + _dot_tn(v, kt)


def _hgrn_chunk_step(q, z, v, lb, gs_ref, mk_ref, d, last_row, st):
    c = q.shape[0]
    n_lvl = mk_ref.shape[1] - 1
    logf, k = _hgrn_gate(z, lb)
    hi, mid = _split2(logf)
    cum = _dot(gs_ref[d, 0:c, :], hi) + _dot(gs_ref[d, 0:c, :], mid)
    dist = _dot(gs_ref[d, c:, :], hi)
    tot = cum[last_row:last_row + 1, :]
    qf = q.astype(F32)
    att = mk_ref[d, 0] * _dot_nt(q, k.astype(BF16))
    for lvl in range(1, n_lvl + 1):
        e = jnp.exp(-jnp.abs(dist[(lvl - 1) * c:lvl * c]))
        att = att + mk_ref[d, lvl] * _dot_nt((qf * e).astype(BF16), (k * e).astype(BF16))
    o = _dot(att.astype(BF16), v) + _dot_nt((qf * jnp.exp(cum)).astype(BF16), st.astype(BF16))
    kt = (k * jnp.exp(tot - cum)).astype(BF16)
    st_new = st * jnp.exp(tot) + _dot_tn(v, kt)
    return o, st_new


def _hgrn_kernel(q_ref, v_ref, zf_ref, zb_ref, g_ref, vc_ref, zfc_ref, zbc_ref, lb_ref, gn_ref,
                 gs_ref, mk_ref, o_ref, acc_ref):
    c = HGRN_CHUNK
    t_len = q_ref.shape[0]
    n = t_len // c
    nc = vc_ref.shape[0] // c
    lbf = lb_ref[0]
    lbb = lb_ref[1]
    acc_ref[...] = jnp.zeros_like(acc_ref)
    st0 = jnp.zeros((HEAD, HEAD), F32)

    def rows(i):
        return pl.ds(pl.multiple_of(i * c, c), c)

    def ctx_body(i, carry):
        sf, sb = carry
        rf = rows(i)
        rb = rows(nc - 1 - i)
        sf = _hgrn_state_step(zfc_ref[rf, :].astype(F32), vc_ref[rf, :], lbf, gs_ref[0, 0:c, :], c - 1, sf)
        sb = _hgrn_state_step(zbc_ref[rb, :].astype(F32), vc_ref[rb, :], lbb, gs_ref[1, 0:c, :], 0, sb)
        return sf, sb

    sf, sb = lax.fori_loop(0, nc, ctx_body, (st0, st0))

    def body(i, carry):
        sf, sb = carry
        rf = rows(i)
        rb = rows(n - 1 - i)
        of, sf = _hgrn_chunk_step(q_ref[rf, :], zf_ref[rf, :].astype(F32), v_ref[rf, :], lbf,
                                  gs_ref, mk_ref, 0, c - 1, sf)
        acc_ref[rf, :] += of
        ob, sb = _hgrn_chunk_step(q_ref[rb, :], zb_ref[rb, :].astype(F32), v_ref[rb, :], lbb,
                                  gs_ref, mk_ref, 1, 0, sb)
        acc_ref[rb, :] += ob
        return sf, sb

    lax.fori_loop(0, n, body, (sf, sb))

    o = acc_ref[...]
    gate = g_ref[...].astype(F32)
    y = o * lax.rsqrt(jnp.mean(o * o, axis=-1, keepdims=True) + NORM_EPS) * gn_ref[...]
    o_ref[...] = (y * (gate * jax.nn.sigmoid(gate))).astype(o_ref.dtype)


def _hgrn_call(p3, pc3, lb, gn, n_heads):
    b, t, _ = p3.shape
    cl = pc3.shape[1]
    h = n_heads
    gs, mk = _hgrn_tables(HGRN_CHUNK)
    gs = jnp.asarray(gs, BF16)
    mk = jnp.asarray(mk, F32)

    def col(seg, rows_):
        return pl.BlockSpec((None, rows_, HEAD), lambda bi, hi, seg=seg: (bi, 0, seg * h + hi))

    return pl.pallas_call(
        _hgrn_kernel,
        grid=(b, h),
        in_specs=[col(0, t), col(1, t), col(2, t), col(3, t), col(4, t),
                  col(1, cl), col(2, cl), col(3, cl),
                  pl.BlockSpec((2, None, 1, HEAD), lambda bi, hi: (0, hi, 0, 0)),
                  pl.BlockSpec((1, HEAD), lambda bi, hi: (0, 0)),
                  pl.BlockSpec(gs.shape, lambda bi, hi: (0, 0, 0)),
                  pl.BlockSpec(mk.shape, lambda bi, hi: (0, 0, 0, 0))],
        out_specs=pl.BlockSpec((None, t, HEAD), lambda bi, hi: (bi, 0, hi)),
        out_shape=jax.ShapeDtypeStruct((b, t, h * HEAD), BF16),
        scratch_shapes=[pltpu.VMEM((t, HEAD), F32)],
        compiler_params=_cparams(("parallel", "parallel")),
        name="hgrn2",
    )(p3, p3, p3, p3, p3, pc3, pc3, pc3, lb, gn, gs, mk)


def _rope_tables(t_len):
    rows = t_len // GRID_W
    row = jnp.repeat(jnp.arange(rows, dtype=F32), GRID_W)
    colp = jnp.tile(jnp.arange(GRID_W, dtype=F32), rows)
    n_freq = DATTN_HEAD_DIM // 4
    inv = ROPE_BASE ** (-jnp.arange(n_freq, dtype=F32) / n_freq)
    ang_r = row[:, None] * inv
    ang_c = colp[:, None] * inv
    cos64 = jnp.concatenate([jnp.cos(ang_r), jnp.cos(ang_r), jnp.cos(ang_c), jnp.cos(ang_c)], axis=1)
    sin64 = jnp.concatenate([-jnp.sin(ang_r), jnp.sin(ang_r), -jnp.sin(ang_c), jnp.sin(ang_c)], axis=1)
    lane = np.arange(HEAD)
    partner = np.where(lane % (2 * n_freq) < n_freq, lane + n_freq, lane - n_freq)
    perm = np.zeros((HEAD, HEAD), np.float32)
    perm[partner, lane] = 1.0
    return jnp.tile(cos64, (1, 2)), jnp.tile(sin64, (1, 2)), jnp.asarray(perm, BF16)


def _attn_kernel(lam_ref, q_ref, k_ref, v_ref, kc_ref, vc_ref, cos_ref, sin_ref, perm_ref, sg_ref,
                 o_ref, kall_ref, vall_ref, *, out_scale):
    qi = pl.program_id(2)
    tq = q_ref.shape[0]
    cl = kc_ref.shape[0]

    @pl.when(qi == 0)
    def _():
        k = k_ref[...]
        kr = k.astype(F32) * cos_ref[...] + _dot(k, perm_ref[...]) * sin_ref[...]
        kall_ref[cl:, :] = kr.astype(BF16)
        kall_ref[0:cl, :] = kc_ref[...]
        vall_ref[cl:, 0:HEAD] = v_ref[...]
        vall_ref[0:cl, 0:HEAD] = vc_ref[...]
        vall_ref[:, HEAD:] = jnp.ones((vall_ref.shape[0], HEAD), BF16)

    r0 = pl.multiple_of(qi * tq, tq)
    q = q_ref[...]
    cq = cos_ref[pl.ds(r0, tq), :]
    sq = sin_ref[pl.ds(r0, tq), :]
    qr = (q.astype(F32) * cq + _dot(q, perm_ref[...]) * sq) * (DATTN_HEAD_DIM ** -0.5)
    lane = lax.broadcasted_iota(jnp.int32, (tq, HEAD), 1)
    q0 = jnp.where(lane < DATTN_HEAD_DIM, qr, 0.0).astype(BF16)
    q1 = jnp.where(lane >= DATTN_HEAD_DIM, qr, 0.0).astype(BF16)
    lam = lam_ref[0]

    def softmax_av(qm):
        s = _dot_nt(qm, kall_ref[...])
        m = jnp.max(s, axis=-1, keepdims=True)
        p = jnp.exp((s - m).astype(BF16))
        oa = _dot(p, vall_ref[...])
        return oa[:, 0:HEAD] / oa[:, HEAD:]

    sub = min(ATTN_SUB_ROWS, tq)
    for r in range(tq // sub):
        rs = slice(r * sub, (r + 1) * sub)
        o = softmax_av(q0[rs]) - lam * softmax_av(q1[rs])
        y = o * lax.rsqrt(jnp.mean(o * o, axis=-1, keepdims=True) + NORM_EPS) * sg_ref[...] * out_scale
        o_ref[rs, :] = y.astype(o_ref.dtype)


def _attn_call(p3, pc3, lam, subln_g, n_hgrn_heads, n_heads, out_scale, tq=512):
    b, t, _ = p3.shape
    cl = pc3.shape[1]
    tq = min(tq, t)
    base = 5 * n_hgrn_heads
    cos_t, sin_t, perm = _rope_tables(t)

    def col(seg, rows_):
        return pl.BlockSpec((None, rows_, HEAD), lambda bi, hi, qi, seg=seg: (bi, 0, base + seg * n_heads + hi))

    full2 = lambda shape: pl.BlockSpec(shape, lambda bi, hi, qi: (0, 0))
    return pl.pallas_call(
        functools.partial(_attn_kernel, out_scale=out_scale),
        grid=(b, n_heads, t // tq),
        in_specs=[pl.BlockSpec(memory_space=pltpu.SMEM),
                  pl.BlockSpec((None, tq, HEAD), lambda bi, hi, qi: (bi, qi, base + hi)),
                  col(1, t), col(2, t), col(1, cl), col(2, cl),
                  full2((t, HEAD)), full2((t, HEAD)), full2((HEAD, HEAD)), full2((1, HEAD))],
        out_specs=pl.BlockSpec((None, tq, HEAD), lambda bi, hi, qi: (bi, qi, hi)),
        out_shape=jax.ShapeDtypeStruct((b, t, n_heads * HEAD), BF16),
        scratch_shapes=[pltpu.VMEM((cl + t, HEAD), BF16), pltpu.VMEM((cl + t, 2 * HEAD), BF16)],
        compiler_params=_cparams(("parallel", "parallel", "arbitrary")),
        name="diff_attn",
    )(lam, p3, p3, p3, pc3, pc3, cos_t, sin_t, perm, subln_g)


def _resid_router_kernel(*refs, n_act, has_bias):
    acts = refs[0:n_act]
    ws = refs[n_act:2 * n_act]
    pos = 2 * n_act
    bias_ref = refs[pos] if has_bias else None
    pos += int(has_bias)
    x_ref, g1_ref, gf_ref, sh_ref, sc_ref, wrh_ref, wrl_ref, br_ref = refs[pos:pos + 8]
    x1_ref, h2_ref, ti_ref, gt_ref = refs[pos + 8:pos + 12]

    y = _dot(acts[0][...], ws[0][...])
    for a, w in zip(acts[1:], ws[1:]):
        y = y + _dot(a[...], w[...])
    if has_bias:
        y = y + bias_ref[...]
    x1 = x_ref[...] + g1_ref[...] * y
    x1_ref[...] = x1
    h2 = _norm_mod(x1, gf_ref[...], sh_ref[...], sc_ref[...])
    hi = h2.astype(BF16)
    hi_f = hi.astype(F32)
    h2_ref[...] = _pack_halves(hi_f)
    lo = (h2 - hi_f).astype(BF16)
    wrh = wrh_ref[...]
    logits = _dot(hi, wrh) + (_dot(lo, wrh) + _dot(hi, wrl_ref[...])) + br_ref[...]

    tm = logits.shape[0]
    lane = lax.broadcasted_iota(jnp.int32, (tm, LANES), 1).astype(F32)
    vals, idxs = [], []
    rem = logits
    for _ in range(TOP_K):
        m = jnp.max(rem, axis=-1, keepdims=True)
        idx = jnp.min(jnp.where(rem == m, lane, float(LANES)), axis=-1, keepdims=True)
        vals.append(m)
        idxs.append(idx)
        rem = jnp.where(lane == idx, -jnp.inf, rem)
    es = [jnp.exp(v - vals[0]) for v in vals]
    inv = 1.0 / (es[0] + es[1] + es[2] + es[3])
    ti = jnp.zeros((tm, LANES), F32)
    gt = jnp.zeros((tm, LANES), F32)
    for k in range(TOP_K):
        ti = jnp.where(lane == float(k), idxs[k], ti)
        gt = jnp.where(lane == float(k), es[k] * inv, gt)
    ti_ref[...] = ti.astype(jnp.int32)
    gt_ref[...] = gt


def _resid_router_call(acts, ws, bias, x2, g1, gf, sh2, sc2, rows_per_mod, w_router, b_router, tm=256):
    m, d = x2.shape
    tm = min(tm, m, rows_per_mod)
    bpm = rows_per_mod // tm
    n_act = len(acts)
    wr = jnp.zeros((d, LANES), F32).at[:, :N_EXPERTS].set(w_router)
    wr_hi = wr.astype(BF16)
    wr_lo = (wr - wr_hi.astype(F32)).astype(BF16)
    br = jnp.full((1, LANES), NEG_BIG, F32).at[0, :N_EXPERTS].set(b_router)
    row = lambda width: pl.BlockSpec((tm, width), lambda i: (i, 0))
    full = lambda shape: pl.BlockSpec(shape, lambda i: (0, 0))
    mod_spec = pl.BlockSpec((None, 1, d), lambda i: (i // bpm, 0, 0))
    in_specs = [row(a.shape[1]) for a in acts] + [full(w.shape) for w in ws]
    args = list(acts) + list(ws)
    if bias is not None:
        in_specs.append(full((1, d)))
        args.append(bias.reshape(1, d))
    in_specs += [row(d), mod_spec, full((1, d)), mod_spec, mod_spec,
                 full((d, LANES)), full((d, LANES)), full((1, LANES))]
    args += [x2, g1, gf, sh2, sc2, wr_hi, wr_lo, br]
    return pl.pallas_call(
        functools.partial(_resid_router_kernel, n_act=n_act, has_bias=bias is not None),
        grid=(m // tm,),
        in_specs=in_specs,
        out_specs=[row(d), row(d // 2), row(LANES), row(LANES)],
        out_shape=[jax.ShapeDtypeStruct((m, d), F32), jax.ShapeDtypeStruct((m, d // 2), jnp.uint32),
                   jax.ShapeDtypeStruct((m, LANES), jnp.int32), jax.ShapeDtypeStruct((m, LANES), F32)],
        compiler_params=_cparams(("parallel",)),
        name="resid_router",
    )(*args)


def _moe_kernel(be_ref, bx_ref, bv_ref, x_ref, wg_ref, wu_ref, bg_ref, bu_ref, wd_ref, bd_ref, o_ref,
                xb_ref, acc_ref):
    i = pl.program_id(0)
    f = pl.program_id(1)
    last = pl.num_programs(1) - 1
    valid = bv_ref[i] > 0
    half = x_ref.shape[1]

    @pl.when(jnp.logical_and(valid, f == 0))
    def _():
        lo, hi = _unpack_halves(x_ref[...])
        xb_ref[:, 0:half] = lo.astype(BF16)
        xb_ref[:, half:] = hi.astype(BF16)
        acc_ref[...] = jnp.broadcast_to(bd_ref[...], acc_ref.shape)

    @pl.when(valid)
    def _():
        x = xb_ref[...]
        g = _dot(x, wg_ref[...].astype(BF16)) + bg_ref[...]
        u = _dot(x, wu_ref[...].astype(BF16)) + bu_ref[...]
        g = jnp.minimum(g, SWIGLU_LIMIT)
        u = jnp.clip(u, -SWIGLU_LIMIT, SWIGLU_LIMIT)
        a = g * jax.nn.sigmoid(SWIGLU_ALPHA * g) * (u + 1.0)
        acc_ref[...] += _dot(a.astype(BF16), wd_ref[...].astype(BF16))

    @pl.when(jnp.logical_and(valid, f == last))
    def _():
        o_ref[...] = _pack_halves(acc_ref[...].astype(BF16).astype(F32))

    @pl.when(jnp.logical_and(jnp.logical_not(valid), f == last))
    def _():
        o_ref[...] = jnp.zeros_like(o_ref)


def _moe_call(xs, blk_e, blk_x, blk_v, layer, w_gu, b_gu, w_down, b_down, bm, tf=256):
    n_slots, half = xs.shape
    d = 2 * half
    n_l, n_e, ff = w_down.shape[0:3]
    tf = min(tf, ff)
    nf = ff // tf
    nb = n_slots // bm
    fsel = lambda f, bv, i: jnp.where(bv[i] > 0, f, nf - 1)
    grid_spec = pltpu.PrefetchScalarGridSpec(
        num_scalar_prefetch=3,
        grid=(nb, nf),
        in_specs=[
            pl.BlockSpec((bm, half), lambda i, f, be, bx, bv: (bx[i], 0)),
            pl.BlockSpec((None, None, d, tf), lambda i, f, be, bx, bv: (layer, be[i], 0, fsel(f, bv, i))),
            pl.BlockSpec((None, None, d, tf), lambda i, f, be, bx, bv: (layer, be[i], 0, fsel(f, bv, i) + nf)),
            pl.BlockSpec((None, None, 1, tf), lambda i, f, be, bx, bv: (layer, be[i], 0, fsel(f, bv, i))),
            pl.BlockSpec((None, None, 1, tf), lambda i, f, be, bx, bv: (layer, be[i], 0, fsel(f, bv, i) + nf)),
            pl.BlockSpec((None, None, tf, d), lambda i, f, be, bx, bv: (layer, be[i], fsel(f, bv, i), 0)),
            pl.BlockSpec((None, None, 1, d), lambda i, f, be, bx, bv: (layer, be[i], 0, 0)),
        ],
        out_specs=pl.BlockSpec((bm, half), lambda i, f, be, bx, bv: (i, 0)),
        scratch_shapes=[pltpu.VMEM((bm, d), BF16), pltpu.VMEM((bm, d), F32)],
    )
    b_gu4 = b_gu.reshape(n_l, n_e, 1, 2 * ff)
    return pl.pallas_call(
        _moe_kernel,
        grid_spec=grid_spec,
        out_shape=jax.ShapeDtypeStruct((n_slots, half), jnp.uint32),
        compiler_params=_cparams(("arbitrary", "arbitrary")),
        name="moe_experts",
    )(blk_e, blk_x, blk_v, xs, w_gu, w_gu, b_gu4, b_gu4, w_down, b_down.reshape(n_l, n_e, 1, d))


def _combine_kernel(x_ref, yg_ref, gt_ref, g2_ref, fg_ref, o_ref, *, final):
    gt = gt_ref[...]
    acc_lo, acc_hi = None, None
    for k in range(TOP_K):
        lo, hi = _unpack_halves(yg_ref[k])
        gk = gt[:, k:k + 1]
        acc_lo = lo * gk if acc_lo is None else acc_lo + lo * gk
        acc_hi = hi * gk if acc_hi is None else acc_hi + hi * gk
    acc = jnp.concatenate([acc_lo, acc_hi], axis=1)
    x2 = x_ref[...] + g2_ref[...] * acc
    if final:
        x2 = x2 * lax.rsqrt(jnp.mean(x2 * x2, axis=-1, keepdims=True) + NORM_EPS) * fg_ref[...]
    o_ref[...] = x2


def _combine_call(x1, yg, gates, g2, rows_per_mod, final_g, final, tm=256):
    m, d = x1.shape
    tm = min(tm, m, rows_per_mod)
    bpm = rows_per_mod // tm
    return pl.pallas_call(
        functools.partial(_combine_kernel, final=final),
        grid=(m // tm,),
        in_specs=[pl.BlockSpec((tm, d), lambda i: (i, 0)),
                  pl.BlockSpec((TOP_K, tm, d // 2), lambda i: (0, i, 0)),
                  pl.BlockSpec((tm, LANES), lambda i: (i, 0)),
                  pl.BlockSpec((None, 1, d), lambda i: (i // bpm, 0, 0)),
                  pl.BlockSpec((1, d), lambda i: (0, 0))],
        out_specs=pl.BlockSpec((tm, d), lambda i: (i, 0)),
        out_shape=jax.ShapeDtypeStruct((m, d), F32),
        compiler_params=_cparams(("parallel",)),
        name="moe_combine",
    )(x1, yg, gates, g2, final_g.reshape(1, d))


def _moe_layer(x1, h2, topi, gates, g2, rows_per_mod, layer, w_gu, b_gu, w_down, b_down, final_g, final, bm):
    n, d = x1.shape
    i32 = jnp.int32
    flat_e = topi[:, :TOP_K].reshape(-1)
    n_asn = n * TOP_K
    experts = jnp.arange(N_EXPERTS, dtype=i32)
    onehot = flat_e[:, None] == experts[None, :]
    counts = jnp.sum(onehot, axis=0, dtype=i32)
    padded = (counts + bm - 1) // bm * bm
    pend = jnp.cumsum(padded)
    pstart = pend - padded
    start = jnp.cumsum(counts) - counts
    order = jnp.argsort(flat_e).astype(i32)
    rank = jnp.argsort(order).astype(i32)
    shift = jnp.sum(jnp.where(onehot, (pstart - start)[None, :], 0), axis=1, dtype=i32)
    dest = rank + shift
    n_blocks = -(-n_asn // bm) + N_EXPERTS
    n_slots = n_blocks * bm
    slot = jnp.arange(n_slots, dtype=i32)
    slot_e = jnp.minimum(jnp.sum(slot[:, None] >= pend[None, :], axis=1, dtype=i32), N_EXPERTS - 1)
    slot_hot = slot_e[:, None] == experts[None, :]
    pick = lambda v: jnp.sum(jnp.where(slot_hot, v[None, :], 0), axis=1, dtype=i32)
    pos = slot - pick(pstart)
    src = jnp.clip(pick(start) + pos, 0, n_asn - 1)
    slot_tok = jnp.where(pos < pick(counts), order[src] // TOP_K, slot % n).astype(i32)
    blk = jnp.arange(n_blocks, dtype=i32)
    n_used = pend[-1] // bm
    blk_v = (blk < n_used).astype(i32)
    blk_x = jnp.minimum(blk, n_used - 1)
    blk_e = jnp.minimum(jnp.sum((blk_x * bm)[:, None] >= pend[None, :], axis=1, dtype=i32), N_EXPERTS - 1)
    xs = _sc_gather(h2, slot_tok)
    ys = _moe_call(xs, blk_e, blk_x, blk_v, layer, w_gu, b_gu, w_down, b_down, bm)
    yg = _sc_gather(ys, dest.reshape(n, TOP_K).T.reshape(-1)).reshape(TOP_K, n, d // 2)
    return _combine_call(x1, yg, gates, g2, rows_per_mod, final_g, final)


def _conv_kernel(prev_ref, cur_ref, next_ref, w_ref, bdw_ref, lg_ref, lb_ref, o_ref, buf_ref, sh_ref, cv_ref):
    i = pl.program_id(1)
    n_i = pl.num_programs(1)
    tr = cur_ref.shape[0]
    d = cur_ref.shape[1]
    hl = CONV_HALO
    pad = CONV_KERNEL // 2
    sub = SUBLANES
    rows_sh = sh_ref.shape[1]
    buf_ref[0:hl, :] = jnp.where(i > 0, prev_ref[...].astype(F32), 0.0)
    buf_ref[hl:hl + tr, :] = cur_ref[...].astype(F32)
    buf_ref[hl + tr:, :] = jnp.where(i < n_i - 1, next_ref[...].astype(F32), 0.0)
    for s in range(1, sub):
        sh_ref[s - 1] = buf_ref[s:s + rows_sh, :]
    s1 = jnp.zeros((tr, LANES), F32)
    for cb in range(d // LANES):
        ls = slice(cb * LANES, (cb + 1) * LANES)
        acc = jnp.zeros((tr, LANES), F32) + bdw_ref[:, ls]
        for k in range(CONV_KERNEL):
            off = hl - pad + k
            base = off // sub * sub
            if off % sub == 0:
                rows = buf_ref[base:base + tr, ls]
            else:
                rows = sh_ref[off % sub - 1, base:base + tr, ls]
            acc = acc + w_ref[k:k + 1, ls] * rows
        cv_ref[:, ls] = acc
        s1 = s1 + acc
    mu = jnp.sum(s1, axis=-1, keepdims=True) * (1.0 / d)
    xc = cv_ref[...] - mu
    var = jnp.mean(xc * xc, axis=-1, keepdims=True)
    y = xc * lax.rsqrt(var + NORM_EPS) * lg_ref[...] + lb_ref[...]
    o_ref[...] = (y * jax.nn.sigmoid(y)).astype(o_ref.dtype)


def _conv_call(a3, w_dw, b_dw, ln_g, ln_b, tr=256):
    b, t, d = a3.shape
    tr = min(tr, t)
    hl = CONV_HALO
    rb = tr // hl
    n_hb = t // hl
    kpad = -(-CONV_KERNEL // 8) * 8
    w = jnp.zeros((kpad, d), F32).at[:CONV_KERNEL].set(w_dw)
    vec = lambda v: v.reshape(1, d)
    full = lambda shape: pl.BlockSpec(shape, lambda bi, i: (0, 0))
    return pl.pallas_call(
        _conv_kernel,
        grid=(b, t // tr),
        in_specs=[pl.BlockSpec((None, hl, d), lambda bi, i: (bi, jnp.maximum(i * rb - 1, 0), 0)),
                  pl.BlockSpec((None, tr, d), lambda bi, i: (bi, i, 0)),
                  pl.BlockSpec((None, hl, d), lambda bi, i: (bi, jnp.minimum((i + 1) * rb, n_hb - 1), 0)),
                  full((kpad, d)), full((1, d)), full((1, d)), full((1, d))],
        out_specs=pl.BlockSpec((None, tr, d), lambda bi, i: (bi, i, 0)),
        out_shape=jax.ShapeDtypeStruct((b, t, d), BF16),
        scratch_shapes=[pltpu.VMEM((tr + 2 * hl, d), F32),
                        pltpu.VMEM((SUBLANES - 1, tr + 2 * hl - SUBLANES, d), F32),
                        pltpu.VMEM((tr, d), F32)],
        compiler_params=_cparams(("parallel", "parallel")),
        name="dwconv_ln_silu",
    )(a3, a3, a3, w, vec(b_dw), vec(ln_g), vec(ln_b))


def _forward(x, c, ctx, c_ctx, norm_mix_g, norm_ffn_g, ada_w, ada_b,
             ev_w_in, ev_w_out, ev_hgrn_lb_logits, ev_hgrn_norm_g, ev_dattn_lambda, ev_dattn_subln_g,
             od_w_pw1, od_b_pw1, od_w_dw, od_b_dw, od_ln_g, od_ln_b, od_w_pw2, od_b_pw2,
             moe_w_router, moe_b_router, moe_w_gu, moe_b_gu, moe_w_down, moe_b_down, final_norm_g,
             moe_bm=1024):
    b, t, d = x.shape
    cl = ctx.shape[1]
    depth = ada_w.shape[0]
    n_hh = (d // 2) // HEAD
    n_ah = (d // 2) // HEAD
    n = b * t

    n_cond = -(-(b + 1) // 8) * 8
    cs = jnp.zeros((n_cond, d), F32).at[:b].set(c).at[b].set(c_ctx)
    mod = _ada_call(cs, ada_w, ada_b)

    x2 = x.reshape(n, d)
    ctx2 = ctx.reshape(b * cl, d)
    for l in range(depth):
        j = l // 2
        parts = [mod[l, :, i * d:(i + 1) * d] for i in range(N_MOD)]
        sh1, sc1, g1, sh2, sc2, g2 = [p[:b].reshape(b, 1, d) for p in parts]
        csh1, csc1 = [p[b:b + 1].reshape(1, 1, d) for p in parts[:2]]
        gmix = norm_mix_g[l].reshape(1, d)
        gffn = norm_ffn_g[l].reshape(1, d)
        if l % 2 == 0:
            w_in = ev_w_in[j].astype(BF16)
            p = _proj_call(x2, gmix, sh1, sc1, t, w_in)
            pc = _proj_call(ctx2, gmix, csh1, csc1, b * cl, w_in)
            p3 = p.reshape(b, t, p.shape[1])
            pc3 = pc.reshape(b, cl, pc.shape[1])
            p_lb = jax.nn.softmax(ev_hgrn_lb_logits.astype(F32), axis=1)
            lb = jnp.cumsum(p_lb, axis=1)[:, j].reshape(2, n_hh, 1, HEAD)
            yh = _hgrn_call(p3, pc3, lb, ev_hgrn_norm_g[j].reshape(1, HEAD), n_hh)
            lam_init = 0.8 - 0.6 * math.exp(-0.3 * l)
            lp = ev_dattn_lambda[j].astype(F32)
            lam = (jnp.exp(jnp.sum(lp[0] * lp[1])) - jnp.exp(jnp.sum(lp[2] * lp[3])) + lam_init).reshape(1)
            ya = _attn_call(p3, pc3, lam, ev_dattn_subln_g[j].reshape(1, HEAD), n_hh, n_ah, 1.0 - lam_init)
            w_out = ev_w_out[j].astype(BF16)
            hw = n_hh * HEAD
            acts = [yh.reshape(n, hw), ya.reshape(n, n_ah * HEAD)]
            ws = [w_out[:hw], w_out[hw:]]
            bias = None
        else:
            a = _proj_call(x2, gmix, sh1, sc1, t, od_w_pw1[j].astype(BF16), glu_bias=od_b_pw1[j])
            a = _conv_call(a.reshape(b, t, d), od_w_dw[j], od_b_dw[j], od_ln_g[j], od_ln_b[j])
            acts = [a.reshape(n, d)]
            ws = [od_w_pw2[j].astype(BF16)]
            bias = od_b_pw2[j]
        x1, h2, topi, gates = _resid_router_call(acts, ws, bias, x2, g1, gffn, sh2, sc2, t,
                                                 moe_w_router[l], moe_b_router[l])
        x2 = _moe_layer(x1, h2, topi, gates, g2, t, l, moe_w_gu, moe_b_gu, moe_w_down, moe_b_down,
                        final_norm_g, l == depth - 1, moe_bm)
    return x2.reshape(b, t, d)


def kernel(x, c, ctx, c_ctx, norm_mix_g, norm_ffn_g, ada_w, ada_b, ev_w_in, ev_w_out, ev_hgrn_lb_logits, ev_hgrn_norm_g, ev_dattn_lambda, ev_dattn_subln_g, od_w_pw1, od_b_pw1, od_w_dw, od_b_dw, od_ln_g, od_ln_b, od_w_pw2, od_b_pw2, moe_w_router, moe_b_router, moe_w_gu, moe_b_gu, moe_w_down, moe_b_down, final_norm_g):
    return _forward(x, c, ctx, c_ctx, norm_mix_g, norm_ffn_g, ada_w, ada_b, ev_w_in, ev_w_out,
                    ev_hgrn_lb_logits, ev_hgrn_norm_g, ev_dattn_lambda, ev_dattn_subln_g,
                    od_w_pw1, od_b_pw1, od_w_dw, od_b_dw, od_ln_g, od_ln_b, od_w_pw2, od_b_pw2,
                    moe_w_router, moe_b_router, moe_w_gu, moe_b_gu, moe_w_down, moe_b_down, final_norm_g)
```

```python
import functools
import math

import numpy as np
import jax
import jax.numpy as jnp
from jax import lax
from jax.experimental import pallas as pl
from jax.experimental.pallas import tpu as pltpu
from jax.experimental.pallas import tpu_sc as plsc

F32 = jnp.float32
BF16 = jnp.bfloat16

NORM_EPS = 1e-6
GRID_W = 64
N_MOD = 6
HEAD = 128
DATTN_HEAD_DIM = 64
ROPE_BASE = 10000.0
CONV_KERNEL = 31
CONV_HALO = 16
N_EXPERTS = 32
TOP_K = 4
SWIGLU_ALPHA = 1.702
SWIGLU_LIMIT = 7.0
LANES = 128
SUBLANES = 8
HGRN_CHUNK = 128
ATTN_SUB_ROWS = 256
NEG_BIG = -1e30
SC_CORES = 2
SC_SUBCORES = 16
SC_GATHER_ROWS = 32
HI_MASK = 0xFFFF0000

VMEM_LIMIT = 56 * 1024 * 1024


def _cparams(sem):
    return pltpu.CompilerParams(dimension_semantics=sem, vmem_limit_bytes=VMEM_LIMIT)


def _dot(a, b):
    return jnp.dot(a, b, preferred_element_type=F32)


def _dot_nt(a, b):
    return lax.dot_general(a, b, (((1,), (1,)), ((), ())), preferred_element_type=F32)


def _dot_tn(a, b):
    return lax.dot_general(a, b, (((0,), (0,)), ((), ())), preferred_element_type=F32)


def _split3(x):
    hi = x.astype(BF16)
    r1 = x - hi.astype(F32)
    mid = r1.astype(BF16)
    lo = (r1 - mid.astype(F32)).astype(BF16)
    return hi, mid, lo


def _norm_mod(x, g, shift, scale):
    ms = jnp.mean(x * x, axis=-1, keepdims=True)
    return (x * lax.rsqrt(ms + NORM_EPS) * g) * (1.0 + scale) + shift


def _pack_halves(a):
    half = a.shape[1] // 2
    lo = pltpu.bitcast(a[:, :half], jnp.uint32) >> 16
    hi = pltpu.bitcast(a[:, half:], jnp.uint32) & jnp.uint32(HI_MASK)
    return hi | lo


def _unpack_halves(p):
    lo = pltpu.bitcast(p << 16, F32)
    hi = pltpu.bitcast(p & jnp.uint32(HI_MASK), F32)
    return lo, hi


def _sc_gather(table, idx):
    m = idx.shape[0]
    w = table.shape[1]
    r = SC_GATHER_ROWS
    n_workers = SC_CORES * SC_SUBCORES
    per_worker = m // n_workers
    n_steps = per_worker // r
    assert per_worker * n_workers == m and n_steps * r == per_worker and n_steps % 2 == 0 and n_steps >= 2
    mesh = plsc.VectorSubcoreMesh(core_axis_name="c", subcore_axis_name="s")

    @functools.partial(
        pl.kernel, mesh=mesh,
        out_type=jax.ShapeDtypeStruct((m, w), table.dtype),
        scratch_types=[pltpu.VMEM((per_worker,), jnp.int32), pltpu.VMEM((2, r, w), table.dtype),
                       pltpu.SemaphoreType.DMA((2,)), pltpu.SemaphoreType.DMA((2,))],
    )
    def gather_kernel(t_hbm, i_hbm, o_hbm, idx_v, rows_v, gsem, osem):
        base = (lax.axis_index("s") * SC_CORES + lax.axis_index("c")) * per_worker
        pltpu.sync_copy(i_hbm.at[pl.ds(base, per_worker)], idx_v)

        def fetch(j, b):
            return pltpu.make_async_copy(t_hbm.at[idx_v.at[pl.ds(j * r, r)]], rows_v.at[b], gsem.at[b])

        def flush(j, b):
            return pltpu.make_async_copy(rows_v.at[b], o_hbm.at[pl.ds(base + j * r, r)], osem.at[b])

        fetch(0, 0).start()

        @pl.loop(0, n_steps, step=2)
        def _(j0):
            for b in range(2):
                j = j0 + b
                fetch(j, b).wait()
                flush(j, b).start()

                @pl.when(j + 1 < n_steps)
                def _():
                    @pl.when(j >= 1)
                    def _():
                        flush(j - 1, 1 - b).wait()

                    fetch(j + 1, 1 - b).start()

        flush(n_steps - 2, 0).wait()
        flush(n_steps - 1, 1).wait()

    return gather_kernel(table, idx)


def _ada_kernel(cs_ref, w_ref, b_ref, o_ref):
    cs = cs_ref[...]
    s = cs * jax.nn.sigmoid(cs)
    hi, mid, lo = _split3(s)
    wh, wm, wl = _split3(w_ref[...])
    acc = _dot(hi, wh) + (_dot(hi, wm) + _dot(mid, wh)) + (_dot(hi, wl) + _dot(mid, wm) + _dot(lo, wh))
    o_ref[...] = acc + b_ref[...]


def _ada_call(cs, ada_w, ada_b):
    n_layers, d, n6 = ada_w.shape
    r = cs.shape[0]
    tn = min(512, n6)
    return pl.pallas_call(
        _ada_kernel,
        grid=(n_layers, n6 // tn),
        in_specs=[
            pl.BlockSpec((r, d), lambda l, j: (0, 0)),
            pl.BlockSpec((None, d, tn), lambda l, j: (l, 0, j)),
            pl.BlockSpec((None, 1, tn), lambda l, j: (l, 0, j)),
        ],
        out_specs=pl.BlockSpec((None, r, tn), lambda l, j: (l, 0, j)),
        out_shape=jax.ShapeDtypeStruct((n_layers, r, n6), F32),
        compiler_params=_cparams(("parallel", "parallel")),
        name="ada_mod",
    )(cs, ada_w, ada_b.reshape(n_layers, 1, n6))


def _proj_kernel(x_ref, g_ref, sh_ref, sc_ref, w_ref, o_ref, h_ref):
    @pl.when(pl.program_id(1) == 0)
    def _():
        h_ref[...] = _norm_mod(x_ref[...], g_ref[...], sh_ref[...], sc_ref[...]).astype(BF16)

    o_ref[...] = _dot(h_ref[...], w_ref[...]).astype(o_ref.dtype)


def _proj_glu_kernel(x_ref, g_ref, sh_ref, sc_ref, wa_ref, wb_ref, ba_ref, bb_ref, o_ref, h_ref):
    @pl.when(pl.program_id(1) == 0)
    def _():
        h_ref[...] = _norm_mod(x_ref[...], g_ref[...], sh_ref[...], sc_ref[...]).astype(BF16)

    h = h_ref[...]
    a = _dot(h, wa_ref[...]) + ba_ref[...]
    b = _dot(h, wb_ref[...]) + bb_ref[...]
    o_ref[...] = (a * jax.nn.sigmoid(b)).astype(o_ref.dtype)


def _proj_call(x2, g, shift, scale, rows_per_mod, w, glu_bias=None, tm=1024, tn=1024):
    m, d = x2.shape
    n_out = w.shape[1] if glu_bias is None else w.shape[1] // 2
    tm = min(tm, m, rows_per_mod)
    tn = min(tn, n_out)
    bpm = rows_per_mod // tm
    row_spec = pl.BlockSpec((tm, d), lambda i, j: (i, 0))
    vec_spec = pl.BlockSpec((1, d), lambda i, j: (0, 0))
    mod_spec = pl.BlockSpec((None, 1, d), lambda i, j: (i // bpm, 0, 0))
    out_spec = pl.BlockSpec((tm, tn), lambda i, j: (i, j))
    common = dict(
        grid=(m // tm, n_out // tn),
        out_specs=out_spec,
        out_shape=jax.ShapeDtypeStruct((m, n_out), BF16),
        scratch_shapes=[pltpu.VMEM((tm, d), BF16)],
        compiler_params=_cparams(("parallel", "arbitrary")),
    )
    if glu_bias is None:
        return pl.pallas_call(
            _proj_kernel,
            in_specs=[row_spec, vec_spec, mod_spec, mod_spec,
                      pl.BlockSpec((d, tn), lambda i, j: (0, j))],
            name="norm_proj",
            **common,
        )(x2, g, shift, scale, w)
    nb = n_out // tn
    bias = glu_bias.reshape(1, 2 * n_out)
    return pl.pallas_call(
        _proj_glu_kernel,
        in_specs=[row_spec, vec_spec, mod_spec, mod_spec,
                  pl.BlockSpec((d, tn), lambda i, j: (0, j)),
                  pl.BlockSpec((d, tn), lambda i, j: (0, j + nb)),
                  pl.BlockSpec((1, tn), lambda i, j: (0, j)),
                  pl.BlockSpec((1, tn), lambda i, j: (0, j + nb))],
        name="norm_proj_glu",
        **common,
    )(x2, g, shift, scale, w, w, bias, bias)


def _hgrn_tables(c):
    n_lvl = int(math.log2(c))
    t = np.arange(c)[:, None]
    s = np.arange(c)[None, :]
    gs = [(s <= t).astype(np.float32)]
    masks = [(s == t).astype(np.float32)]
    for lvl in range(1, n_lvl + 1):
        b = 2 ** (lvl - 1)
        mid = (t // (2 * b)) * (2 * b) + b - 1
        upper = t > mid
        g = np.where(upper & (s > mid) & (s <= t), 1.0, 0.0) - np.where((~upper) & (s > t) & (s <= mid), 1.0, 0.0)
        same = (t // (2 * b)) == (s // (2 * b))
        gs.append(g.astype(np.float32))
        masks.append((same & upper & (s <= mid)).astype(np.float32))
    gs_f = np.concatenate(gs, axis=0)
    mk_f = np.stack(masks, axis=0)
    gs_b = np.concatenate([g[::-1, ::-1] for g in gs], axis=0)
    mk_b = np.stack([m[::-1, ::-1] for m in masks], axis=0)
    return np.stack([gs_f, gs_b]), np.stack([mk_f, mk_b])


def _hgrn_gate(z, lb):
    e = jnp.exp(-jnp.abs(z))
    r = 1.0 / (1.0 + e)
    er = e * r
    pos = z >= 0
    sig = jnp.where(pos, r, er)
    sig_neg = jnp.where(pos, er, r)
    logf = jnp.log(lb + (1.0 - lb) * sig)
    return logf, (1.0 - lb) * sig_neg


def _split2(x):
    hi = x.astype(BF16)
    return hi, (x - hi.astype(F32)).astype(BF16)


def _hgrn_state_step(z, v, lb, tri, last_row, st):
    logf, k = _hgrn_gate(z, lb)
    hi, mid = _split2(logf)
    cum = _dot(tri, hi) + _dot(tri, mid)
    tot = cum[last_row:last_row + 1, :]
    kt = (k * jnp.exp(tot - cum)).astype(BF16)
    return st * jnp.exp(tot) + _dot_tn(v, kt)


def _hgrn_chunk_step(q, z, v, lb, gs_ref, mk_ref, d, last_row, st):
    c = q.shape[0]
    n_lvl = mk_ref.shape[1] - 1
    logf, k = _hgrn_gate(z, lb)
    hi, mid = _split2(logf)
    cum = _dot(gs_ref[d, 0:c, :], hi) + _dot(gs_ref[d, 0:c, :], mid)
    dist = _dot(gs_ref[d, c:, :], hi)
    tot = cum[last_row:last_row + 1, :]
    qf = q.astype(F32)
    att = mk_ref[d, 0] * _dot_nt(q, k.astype(BF16))
    for lvl in range(1, n_lvl + 1):
        e = jnp.exp(-jnp.abs(dist[(lvl - 1) * c:lvl * c]))
        att = att + mk_ref[d, lvl] * _dot_nt((qf * e).astype(BF16), (k * e).astype(BF16))
    o = _dot(att.astype(BF16), v) + _dot_nt((qf * jnp.exp(cum)).astype(BF16), st.astype(BF16))
    kt = (k * jnp.exp(tot - cum)).astype(BF16)
    st_new = st * jnp.exp(tot) + _dot_tn(v, kt)
    return o, st_new


def _hgrn_kernel(q_ref, v_ref, zf_ref, zb_ref, g_ref, vc_ref, zfc_ref, zbc_ref, lb_ref, gn_ref,
                 gs_ref, mk_ref, o_ref, acc_ref):
    c = HGRN_CHUNK
    t_len = q_ref.shape[0]
    n = t_len // c
    nc = vc_ref.shape[0] // c
    lbf = lb_ref[0]
    lbb = lb_ref[1]
    acc_ref[...] = jnp.zeros_like(acc_ref)
    st0 = jnp.zeros((HEAD, HEAD), F32)

    def rows(i):
        return pl.ds(pl.multiple_of(i * c, c), c)

    def ctx_body(i, carry):
        sf, sb = carry
        rf = rows(i)
        rb = rows(nc - 1 - i)
        sf = _hgrn_state_step(zfc_ref[rf, :].astype(F32), vc_ref[rf, :], lbf, gs_ref[0, 0:c, :], c - 1, sf)
        sb = _hgrn_state_step(zbc_ref[rb, :].astype(F32), vc_ref[rb, :], lbb, gs_ref[1, 0:c, :], 0, sb)
        return sf, sb

    sf, sb = lax.fori_loop(0, nc, ctx_body, (st0, st0))

    def body(i, carry):
        sf, sb = carry
        rf = rows(i)
        rb = rows(n - 1 - i)
        of, sf = _hgrn_chunk_step(q_ref[rf, :], zf_ref[rf, :].astype(F32), v_ref[rf, :], lbf,
                                  gs_ref, mk_ref, 0, c - 1, sf)
        acc_ref[rf, :] += of
        ob, sb = _hgrn_chunk_step(q_ref[rb, :], zb_ref[rb, :].astype(F32), v_ref[rb, :], lbb,
                                  gs_ref, mk_ref, 1, 0, sb)
        acc_ref[rb, :] += ob
        return sf, sb

    lax.fori_loop(0, n, body, (sf, sb))

    o = acc_ref[...]
    gate = g_ref[...].astype(F32)
    y = o * lax.rsqrt(jnp.mean(o * o, axis=-1, keepdims=True) + NORM_EPS) * gn_ref[...]
    o_ref[...] = (y * (gate * jax.nn.sigmoid(gate))).astype(o_ref.dtype)


def _hgrn_call(p3, pc3, lb, gn, n_heads):
    b, t, _ = p3.shape
    cl = pc3.shape[1]
    h = n_heads
    gs, mk = _hgrn_tables(HGRN_CHUNK)
    gs = jnp.asarray(gs, BF16)
    mk = jnp.asarray(mk, F32)

    def col(seg, rows_):
        return pl.BlockSpec((None, rows_, HEAD), lambda bi, hi, seg=seg: (bi, 0, seg * h + hi))

    return pl.pallas_call(
        _hgrn_kernel,
        grid=(b, h),
        in_specs=[col(0, t), col(1, t), col(2, t), col(3, t), col(4, t),
                  col(1, cl), col(2, cl), col(3, cl),
                  pl.BlockSpec((2, None, 1, HEAD), lambda bi, hi: (0, hi, 0, 0)),
                  pl.BlockSpec((1, HEAD), lambda bi, hi: (0, 0)),
                  pl.BlockSpec(gs.shape, lambda bi, hi: (0, 0, 0)),
                  pl.BlockSpec(mk.shape, lambda bi, hi: (0, 0, 0, 0))],
        out_specs=pl.BlockSpec((None, t, HEAD), lambda bi, hi: (bi, 0, hi)),
        out_shape=jax.ShapeDtypeStruct((b, t, h * HEAD), BF16),
        scratch_shapes=[pltpu.VMEM((t, HEAD), F32)],
        compiler_params=_cparams(("parallel", "parallel")),
        name="hgrn2",
    )(p3, p3, p3, p3, p3, pc3, pc3, pc3, lb, gn, gs, mk)


def _rope_tables(t_len):
    rows = t_len // GRID_W
    row = jnp.repeat(jnp.arange(rows, dtype=F32), GRID_W)
    colp = jnp.tile(jnp.arange(GRID_W, dtype=F32), rows)
    n_freq = DATTN_HEAD_DIM // 4
    inv = ROPE_BASE ** (-jnp.arange(n_freq, dtype=F32) / n_freq)
    ang_r = row[:, None] * inv
    ang_c = colp[:, None] * inv
    cos64 = jnp.concatenate([jnp.cos(ang_r), jnp.cos(ang_r), jnp.cos(ang_c), jnp.cos(ang_c)], axis=1)
    sin64 = jnp.concatenate([-jnp.sin(ang_r), jnp.sin(ang_r), -jnp.sin(ang_c), jnp.sin(ang_c)], axis=1)
    lane = np.arange(HEAD)
    partner = np.where(lane % (2 * n_freq) < n_freq, lane + n_freq, lane - n_freq)
    perm = np.zeros((HEAD, HEAD), np.float32)
    perm[partner, lane] = 1.0
    return jnp.tile(cos64, (1, 2)), jnp.tile(sin64, (1, 2)), jnp.asarray(perm, BF16)


def _attn_kernel(lam_ref, q_ref, k_ref, v_ref, kc_ref, vc_ref, cos_ref, sin_ref, perm_ref, sg_ref,
                 o_ref, kall_ref, vall_ref, *, out_scale):
    qi = pl.program_id(2)
    tq = q_ref.shape[0]
    cl = kc_ref.shape[0]

    @pl.when(qi == 0)
    def _():
        k = k_ref[...]
        kr = k.astype(F32) * cos_ref[...] + _dot(k, perm_ref[...]) * sin_ref[...]
        kall_ref[cl:, :] = kr.astype(BF16)
        kall_ref[0:cl, :] = kc_ref[...]
        vall_ref[cl:, 0:HEAD] = v_ref[...]
        vall_ref[0:cl, 0:HEAD] = vc_ref[...]
        vall_ref[:, HEAD:] = jnp.ones((vall_ref.shape[0], HEAD), BF16)

    r0 = pl.multiple_of(qi * tq, tq)
    q = q_ref[...]
    cq = cos_ref[pl.ds(r0, tq), :]
    sq = sin_ref[pl.ds(r0, tq), :]
    qr = (q.astype(F32) * cq + _dot(q, perm_ref[...]) * sq) * (DATTN_HEAD_DIM ** -0.5)
    lane = lax.broadcasted_iota(jnp.int32, (tq, HEAD), 1)
    q0 = jnp.where(lane < DATTN_HEAD_DIM, qr, 0.0).astype(BF16)
    q1 = jnp.where(lane >= DATTN_HEAD_DIM, qr, 0.0).astype(BF16)
    lam = lam_ref[0]

    def softmax_av(qm):
        s = _dot_nt(qm, kall_ref[...])
        m = jnp.max(s, axis=-1, keepdims=True)
        p = jnp.exp((s - m).astype(BF16))
        oa = _dot(p, vall_ref[...])
        return oa[:, 0:HEAD] / oa[:, HEAD:]

    sub = min(ATTN_SUB_ROWS, tq)
    for r in range(tq // sub):
        rs = slice(r * sub, (r + 1) * sub)
        o = softmax_av(q0[rs]) - lam * softmax_av(q1[rs])
        y = o * lax.rsqrt(jnp.mean(o * o, axis=-1, keepdims=True) + NORM_EPS) * sg_ref[...] * out_scale
        o_ref[rs, :] = y.astype(o_ref.dtype)


def _attn_call(p3, pc3, lam, subln_g, n_hgrn_heads, n_heads, out_scale, tq=2048):
    b, t, _ = p3.shape
    cl = pc3.shape[1]
    tq = min(tq, t)
    base = 5 * n_hgrn_heads
    cos_t, sin_t, perm = _rope_tables(t)

    def col(seg, rows_):
        return pl.BlockSpec((None, rows_, HEAD), lambda bi, hi, qi, seg=seg: (bi, 0, base + seg * n_heads + hi))

    full2 = lambda shape: pl.BlockSpec(shape, lambda bi, hi, qi: (0, 0))
    return pl.pallas_call(
        functools.partial(_attn_kernel, out_scale=out_scale),
        grid=(b, n_heads, t // tq),
        in_specs=[pl.BlockSpec(memory_space=pltpu.SMEM),
                  pl.BlockSpec((None, tq, HEAD), lambda bi, hi, qi: (bi, qi, base + hi)),
                  col(1, t), col(2, t), col(1, cl), col(2, cl),
                  full2((t, HEAD)), full2((t, HEAD)), full2((HEAD, HEAD)), full2((1, HEAD))],
        out_specs=pl.BlockSpec((None, tq, HEAD), lambda bi, hi, qi: (bi, qi, hi)),
        out_shape=jax.ShapeDtypeStruct((b, t, n_heads * HEAD), BF16),
        scratch_shapes=[pltpu.VMEM((cl + t, HEAD), BF16), pltpu.VMEM((cl + t, 2 * HEAD), BF16)],
        compiler_params=_cparams(("parallel", "parallel", "arbitrary")),
        name="diff_attn",
    )(lam, p3, p3, p3, pc3, pc3, cos_t, sin_t, perm, subln_g)


def _resid_router_kernel(*refs, n_act, has_bias):
    acts = refs[0:n_act]
    ws = refs[n_act:2 * n_act]
    pos = 2 * n_act
    bias_ref = refs[pos] if has_bias else None
    pos += int(has_bias)
    x_ref, g1_ref, gf_ref, sh_ref, sc_ref, wrh_ref, wrl_ref, br_ref = refs[pos:pos + 8]
    x1_ref, h2_ref, ti_ref, gt_ref = refs[pos + 8:pos + 12]

    y = _dot(acts[0][...], ws[0][...])
    for a, w in zip(acts[1:], ws[1:]):
        y = y + _dot(a[...], w[...])
    if has_bias:
        y = y + bias_ref[...]
    x1 = x_ref[...] + g1_ref[...] * y
    x1_ref[...] = x1
    h2 = _norm_mod(x1, gf_ref[...], sh_ref[...], sc_ref[...])
    hi = h2.astype(BF16)
    hi_f = hi.astype(F32)
    h2_ref[...] = _pack_halves(hi_f)
    lo = (h2 - hi_f).astype(BF16)
    wrh = wrh_ref[...]
    logits = _dot(hi, wrh) + (_dot(lo, wrh) + _dot(hi, wrl_ref[...])) + br_ref[...]

    tm = logits.shape[0]
    lane = lax.broadcasted_iota(jnp.int32, (tm, LANES), 1).astype(F32)
    vals, idxs = [], []
    rem = logits
    for _ in range(TOP_K):
        m = jnp.max(rem, axis=-1, keepdims=True)
        idx = jnp.min(jnp.where(rem == m, lane, float(LANES)), axis=-1, keepdims=True)
        vals.append(m)
        idxs.append(idx)
        rem = jnp.where(lane == idx, -jnp.inf, rem)
    es = [jnp.exp(v - vals[0]) for v in vals]
    inv = 1.0 / (es[0] + es[1] + es[2] + es[3])
    ti = jnp.zeros((tm, LANES), F32)
    gt = jnp.zeros((tm, LANES), F32)
    for k in range(TOP_K):
        ti = jnp.where(lane == float(k), idxs[k], ti)
        gt = jnp.where(lane == float(k), es[k] * inv, gt)
    ti_ref[...] = ti.astype(jnp.int32)
    gt_ref[...] = gt


def _resid_router_call(acts, ws, bias, x2, g1, gf, sh2, sc2, rows_per_mod, w_router, b_router, tm=256):
    m, d = x2.shape
    tm = min(tm, m, rows_per_mod)
    bpm = rows_per_mod // tm
    n_act = len(acts)
    wr = jnp.zeros((d, LANES), F32).at[:, :N_EXPERTS].set(w_router)
    wr_hi = wr.astype(BF16)
    wr_lo = (wr - wr_hi.astype(F32)).astype(BF16)
    br = jnp.full((1, LANES), NEG_BIG, F32).at[0, :N_EXPERTS].set(b_router)
    row = lambda width: pl.BlockSpec((tm, width), lambda i: (i, 0))
    full = lambda shape: pl.BlockSpec(shape, lambda i: (0, 0))
    mod_spec = pl.BlockSpec((None, 1, d), lambda i: (i // bpm, 0, 0))
    in_specs = [row(a.shape[1]) for a in acts] + [full(w.shape) for w in ws]
    args = list(acts) + list(ws)
    if bias is not None:
        in_specs.append(full((1, d)))
        args.append(bias.reshape(1, d))
    in_specs += [row(d), mod_spec, full((1, d)), mod_spec, mod_spec,
                 full((d, LANES)), full((d, LANES)), full((1, LANES))]
    args += [x2, g1, gf, sh2, sc2, wr_hi, wr_lo, br]
    return pl.pallas_call(
        functools.partial(_resid_router_kernel, n_act=n_act, has_bias=bias is not None),
        grid=(m // tm,),
        in_specs=in_specs,
        out_specs=[row(d), row(d // 2), row(LANES), row(LANES)],
        out_shape=[jax.ShapeDtypeStruct((m, d), F32), jax.ShapeDtypeStruct((m, d // 2), jnp.uint32),
                   jax.ShapeDtypeStruct((m, LANES), jnp.int32), jax.ShapeDtypeStruct((m, LANES), F32)],
        compiler_params=_cparams(("parallel",)),
        name="resid_router",
    )(*args)


def _moe_kernel(be_ref, bx_ref, bv_ref, x_ref, wg_ref, wu_ref, bg_ref, bu_ref, wd_ref, bd_ref, o_ref,
                xb_ref, acc_ref):
    i = pl.program_id(0)
    f = pl.program_id(1)
    last = pl.num_programs(1) - 1
    valid = bv_ref[i] > 0
    half = x_ref.shape[1]

    @pl.when(jnp.logical_and(valid, f == 0))
    def _():
        lo, hi = _unpack_halves(x_ref[...])
        xb_ref[:, 0:half] = lo.astype(BF16)
        xb_ref[:, half:] = hi.astype(BF16)
        acc_ref[...] = jnp.broadcast_to(bd_ref[...], acc_ref.shape)

    @pl.when(valid)
    def _():
        x = xb_ref[...]
        g = _dot(x, wg_ref[...].astype(BF16)) + bg_ref[...]
        u = _dot(x, wu_ref[...].astype(BF16)) + bu_ref[...]
        g = jnp.minimum(g, SWIGLU_LIMIT)
        u = jnp.clip(u, -SWIGLU_LIMIT, SWIGLU_LIMIT)
        a = g * jax.nn.sigmoid(SWIGLU_ALPHA * g) * (u + 1.0)
        acc_ref[...] += _dot(a.astype(BF16), wd_ref[...].astype(BF16))

    @pl.when(jnp.logical_and(valid, f == last))
    def _():
        o_ref[...] = _pack_halves(acc_ref[...].astype(BF16).astype(F32))

    @pl.when(jnp.logical_and(jnp.logical_not(valid), f == last))
    def _():
        o_ref[...] = jnp.zeros_like(o_ref)


def _moe_call(xs, blk_e, blk_x, blk_v, layer, w_gu, b_gu, w_down, b_down, bm, tf=256):
    n_slots, half = xs.shape
    d = 2 * half
    n_l, n_e, ff = w_down.shape[0:3]
    tf = min(tf, ff)
    nf = ff // tf
    nb = n_slots // bm
    fsel = lambda f, bv, i: jnp.where(bv[i] > 0, f, nf - 1)
    grid_spec = pltpu.PrefetchScalarGridSpec(
        num_scalar_prefetch=3,
        grid=(nb, nf),
        in_specs=[
            pl.BlockSpec((bm, half), lambda i, f, be, bx, bv: (bx[i], 0)),
            pl.BlockSpec((None, None, d, tf), lambda i, f, be, bx, bv: (layer, be[i], 0, fsel(f, bv, i))),
            pl.BlockSpec((None, None, d, tf), lambda i, f, be, bx, bv: (layer, be[i], 0, fsel(f, bv, i) + nf)),
            pl.BlockSpec((None, None, 1, tf), lambda i, f, be, bx, bv: (layer, be[i], 0, fsel(f, bv, i))),
            pl.BlockSpec((None, None, 1, tf), lambda i, f, be, bx, bv: (layer, be[i], 0, fsel(f, bv, i) + nf)),
            pl.BlockSpec((None, None, tf, d), lambda i, f, be, bx, bv: (layer, be[i], fsel(f, bv, i), 0)),
            pl.BlockSpec((None, None, 1, d), lambda i, f, be, bx, bv: (layer, be[i], 0, 0)),
        ],
        out_specs=pl.BlockSpec((bm, half), lambda i, f, be, bx, bv: (i, 0)),
        scratch_shapes=[pltpu.VMEM((bm, d), BF16), pltpu.VMEM((bm, d), F32)],
    )
    b_gu4 = b_gu.reshape(n_l, n_e, 1, 2 * ff)
    return pl.pallas_call(
        _moe_kernel,
        grid_spec=grid_spec,
        out_shape=jax.ShapeDtypeStruct((n_slots, half), jnp.uint32),
        compiler_params=_cparams(("arbitrary", "arbitrary")),
        name="moe_experts",
    )(blk_e, blk_x, blk_v, xs, w_gu, w_gu, b_gu4, b_gu4, w_down, b_down.reshape(n_l, n_e, 1, d))


def _combine_kernel(x_ref, yg_ref, gt_ref, g2_ref, fg_ref, o_ref, *, final):
    gt = gt_ref[...]
    acc_lo, acc_hi = None, None
    for k in range(TOP_K):
        lo, hi = _unpack_halves(yg_ref[k])
        gk = gt[:, k:k + 1]
        acc_lo = lo * gk if acc_lo is None else acc_lo + lo * gk
        acc_hi = hi * gk if acc_hi is None else acc_hi + hi * gk
    acc = jnp.concatenate([acc_lo, acc_hi], axis=1)
    x2 = x_ref[...] + g2_ref[...] * acc
    if final:
        x2 = x2 * lax.rsqrt(jnp.mean(x2 * x2, axis=-1, keepdims=True) + NORM_EPS) * fg_ref[...]
    o_ref[...] = x2


def _combine_call(x1, yg, gates, g2, rows_per_mod, final_g, final, tm=256):
    m, d = x1.shape
    tm = min(tm, m, rows_per_mod)
    bpm = rows_per_mod // tm
    return pl.pallas_call(
        functools.partial(_combine_kernel, final=final),
        grid=(m // tm,),
        in_specs=[pl.BlockSpec((tm, d), lambda i: (i, 0)),
                  pl.BlockSpec((TOP_K, tm, d // 2), lambda i: (0, i, 0)),
                  pl.BlockSpec((tm, LANES), lambda i: (i, 0)),
                  pl.BlockSpec((None, 1, d), lambda i: (i // bpm, 0, 0)),
                  pl.BlockSpec((1, d), lambda i: (0, 0))],
        out_specs=pl.BlockSpec((tm, d), lambda i: (i, 0)),
        out_shape=jax.ShapeDtypeStruct((m, d), F32),
        compiler_params=_cparams(("parallel",)),
        name="moe_combine",
    )(x1, yg, gates, g2, final_g.reshape(1, d))


def _moe_layer(x1, h2, topi, gates, g2, rows_per_mod, layer, w_gu, b_gu, w_down, b_down, final_g, final, bm):
    n, d = x1.shape
    i32 = jnp.int32
    flat_e = topi[:, :TOP_K].reshape(-1)
    n_asn = n * TOP_K
    experts = jnp.arange(N_EXPERTS, dtype=i32)
    onehot = flat_e[:, None] == experts[None, :]
    counts = jnp.sum(onehot, axis=0, dtype=i32)
    padded = (counts + bm - 1) // bm * bm
    pend = jnp.cumsum(padded)
    pstart = pend - padded
    start = jnp.cumsum(counts) - counts
    order = jnp.argsort(flat_e).astype(i32)
    rank = jnp.argsort(order).astype(i32)
    shift = jnp.sum(jnp.where(onehot, (pstart - start)[None, :], 0), axis=1, dtype=i32)
    dest = rank + shift
    n_blocks = -(-n_asn // bm) + N_EXPERTS
    n_slots = n_blocks * bm
    slot = jnp.arange(n_slots, dtype=i32)
    slot_e = jnp.minimum(jnp.sum(slot[:, None] >= pend[None, :], axis=1, dtype=i32), N_EXPERTS - 1)
    slot_hot = slot_e[:, None] == experts[None, :]
    pick = lambda v: jnp.sum(jnp.where(slot_hot, v[None, :], 0), axis=1, dtype=i32)
    pos = slot - pick(pstart)
    src = jnp.clip(pick(start) + pos, 0, n_asn - 1)
    slot_tok = jnp.where(pos < pick(counts), order[src] // TOP_K, slot % n).astype(i32)
    blk = jnp.arange(n_blocks, dtype=i32)
    n_used = pend[-1] // bm
    blk_x = jnp.minimum(blk, n_used - 1)
    blk_e = jnp.minimum(jnp.sum((blk_x * bm)[:, None] >= pend[None, :], axis=1, dtype=i32), N_EXPERTS - 1)
    blk_hot = blk_e[:, None] == experts[None, :]
    blk_pick = lambda v: jnp.sum(jnp.where(blk_hot, v[None, :], 0), axis=1, dtype=i32)
    blk_v = jnp.where(blk < n_used, jnp.clip(blk_pick(counts) - (blk * bm - blk_pick(pstart)), 0, bm), 0).astype(i32)
    xs = _sc_gather(h2, slot_tok)
    ys = _moe_call(xs, blk_e, blk_x, blk_v, layer, w_gu, b_gu, w_down, b_down, bm)
    yg = _sc_gather(ys, dest.reshape(n, TOP_K).T.reshape(-1)).reshape(TOP_K, n, d // 2)
    return _combine_call(x1, yg, gates, g2, rows_per_mod, final_g, final)


def _conv_kernel(prev_ref, cur_ref, next_ref, w_ref, bdw_ref, lg_ref, lb_ref, o_ref, buf_ref, sh_ref, cv_ref):
    i = pl.program_id(1)
    n_i = pl.num_programs(1)
    tr = cur_ref.shape[0]
    d = cur_ref.shape[1]
    hl = CONV_HALO
    pad = CONV_KERNEL // 2
    sub = SUBLANES
    rows_sh = sh_ref.shape[1]
    buf_ref[0:hl, :] = jnp.where(i > 0, prev_ref[...].astype(F32), 0.0)
    buf_ref[hl:hl + tr, :] = cur_ref[...].astype(F32)
    buf_ref[hl + tr:, :] = jnp.where(i < n_i - 1, next_ref[...].astype(F32), 0.0)
    for s in range(1, sub):
        sh_ref[s - 1] = buf_ref[s:s + rows_sh, :]
    s1 = jnp.zeros((tr, LANES), F32)
    for cb in range(d // LANES):
        ls = slice(cb * LANES, (cb + 1) * LANES)
        acc = jnp.zeros((tr, LANES), F32) + bdw_ref[:, ls]
        for k in range(CONV_KERNEL):
            off = hl - pad + k
            base = off // sub * sub
            if off % sub == 0:
                rows = buf_ref[base:base + tr, ls]
            else:
                rows = sh_ref[off % sub - 1, base:base + tr, ls]
            acc = acc + w_ref[k:k + 1, ls] * rows
        cv_ref[:, ls] = acc
        s1 = s1 + acc
    mu = jnp.sum(s1, axis=-1, keepdims=True) * (1.0 / d)
    xc = cv_ref[...] - mu
    var = jnp.mean(xc * xc, axis=-1, keepdims=True)
    y = xc * lax.rsqrt(var + NORM_EPS) * lg_ref[...] + lb_ref[...]
    o_ref[...] = (y * jax.nn.sigmoid(y)).astype(o_ref.dtype)


def _conv_call(a3, w_dw, b_dw, ln_g, ln_b, tr=256):
    b, t, d = a3.shape
    tr = min(tr, t)
    hl = CONV_HALO
    rb = tr // hl
    n_hb = t // hl
    kpad = -(-CONV_KERNEL // 8) * 8
    w = jnp.zeros((kpad, d), F32).at[:CONV_KERNEL].set(w_dw)
    vec = lambda v: v.reshape(1, d)
    full = lambda shape: pl.BlockSpec(shape, lambda bi, i: (0, 0))
    return pl.pallas_call(
        _conv_kernel,
        grid=(b, t // tr),
        in_specs=[pl.BlockSpec((None, hl, d), lambda bi, i: (bi, jnp.maximum(i * rb - 1, 0), 0)),
                  pl.BlockSpec((None, tr, d), lambda bi, i: (bi, i, 0)),
                  pl.BlockSpec((None, hl, d), lambda bi, i: (bi, jnp.minimum((i + 1) * rb, n_hb - 1), 0)),
                  full((kpad, d)), full((1, d)), full((1, d)), full((1, d))],
        out_specs=pl.BlockSpec((None, tr, d), lambda bi, i: (bi, i, 0)),
        out_shape=jax.ShapeDtypeStruct((b, t, d), BF16),
        scratch_shapes=[pltpu.VMEM((tr + 2 * hl, d), F32),
                        pltpu.VMEM((SUBLANES - 1, tr + 2 * hl - SUBLANES, d), F32),
                        pltpu.VMEM((tr, d), F32)],
        compiler_params=_cparams(("parallel", "parallel")),
        name="dwconv_ln_silu",
    )(a3, a3, a3, w, vec(b_dw), vec(ln_g), vec(ln_b))


def _forward(x, c, ctx, c_ctx, norm_mix_g, norm_ffn_g, ada_w, ada_b,
             ev_w_in, ev_w_out, ev_hgrn_lb_logits, ev_hgrn_norm_g, ev_dattn_lambda, ev_dattn_subln_g,
             od_w_pw1, od_b_pw1, od_w_dw, od_b_dw, od_ln_g, od_ln_b, od_w_pw2, od_b_pw2,
             moe_w_router, moe_b_router, moe_w_gu, moe_b_gu, moe_w_down, moe_b_down, final_norm_g,
             moe_bm=1024):
    b, t, d = x.shape
    cl = ctx.shape[1]
    depth = ada_w.shape[0]
    n_hh = (d // 2) // HEAD
    n_ah = (d // 2) // HEAD
    n = b * t

    n_cond = -(-(b + 1) // 8) * 8
    cs = jnp.zeros((n_cond, d), F32).at[:b].set(c).at[b].set(c_ctx)
    mod = _ada_call(cs, ada_w, ada_b)

    x2 = x.reshape(n, d)
    ctx2 = ctx.reshape(b * cl, d)
    for l in range(depth):
        j = l // 2
        parts = [mod[l, :, i * d:(i + 1) * d] for i in range(N_MOD)]
        sh1, sc1, g1, sh2, sc2, g2 = [p[:b].reshape(b, 1, d) for p in parts]
        csh1, csc1 = [p[b:b + 1].reshape(1, 1, d) for p in parts[:2]]
        gmix = norm_mix_g[l].reshape(1, d)
        gffn = norm_ffn_g[l].reshape(1, d)
        if l % 2 == 0:
            w_in = ev_w_in[j].astype(BF16)
            p = _proj_call(x2, gmix, sh1, sc1, t, w_in)
            pc = _proj_call(ctx2, gmix, csh1, csc1, b * cl, w_in)
            p3 = p.reshape(b, t, p.shape[1])
            pc3 = pc.reshape(b, cl, pc.shape[1])
            p_lb = jax.nn.softmax(ev_hgrn_lb_logits.astype(F32), axis=1)
            lb = jnp.cumsum(p_lb, axis=1)[:, j].reshape(2, n_hh, 1, HEAD)
            yh = _hgrn_call(p3, pc3, lb, ev_hgrn_norm_g[j].reshape(1, HEAD), n_hh)
            lam_init = 0.8 - 0.6 * math.exp(-0.3 * l)
            lp = ev_dattn_lambda[j].astype(F32)
            lam = (jnp.exp(jnp.sum(lp[0] * lp[1])) - jnp.exp(jnp.sum(lp[2] * lp[3])) + lam_init).reshape(1)
            ya = _attn_call(p3, pc3, lam, ev_dattn_subln_g[j].reshape(1, HEAD), n_hh, n_ah, 1.0 - lam_init)
            w_out = ev_w_out[j].astype(BF16)
            hw = n_hh * HEAD
            acts = [yh.reshape(n, hw), ya.reshape(n, n_ah * HEAD)]
            ws = [w_out[:hw], w_out[hw:]]
            bias = None
        else:
            a = _proj_call(x2, gmix, sh1, sc1, t, od_w_pw1[j].astype(BF16), glu_bias=od_b_pw1[j])
            a = _conv_call(a.reshape(b, t, d), od_w_dw[j], od_b_dw[j], od_ln_g[j], od_ln_b[j])
            acts = [a.reshape(n, d)]
            ws = [od_w_pw2[j].astype(BF16)]
            bias = od_b_pw2[j]
        x1, h2, topi, gates = _resid_router_call(acts, ws, bias, x2, g1, gffn, sh2, sc2, t,
                                                 moe_w_router[l], moe_b_router[l])
        x2 = _moe_layer(x1, h2, topi, gates, g2, t, l, moe_w_gu, moe_b_gu, moe_w_down, moe_b_down,
                        final_norm_g, l == depth - 1, moe_bm)
    return x2.reshape(b, t, d)


def kernel(x, c, ctx, c_ctx, norm_mix_g, norm_ffn_g, ada_w, ada_b, ev_w_in, ev_w_out, ev_hgrn_lb_logits, ev_hgrn_norm_g, ev_dattn_lambda, ev_dattn_subln_g, od_w_pw1, od_b_pw1, od_w_dw, od_b_dw, od_ln_g, od_ln_b, od_w_pw2, od_b_pw2, moe_w_router, moe_b_router, moe_w_gu, moe_b_gu, moe_w_down, moe_b_down, final_norm_g):
    return _forward(x, c, ctx, c_ctx, norm_mix_g, norm_ffn_g, ada_w, ada_b, ev_w_in, ev_w_out,
                    ev_hgrn_lb_logits, ev_hgrn_norm_g, ev_dattn_lambda, ev_dattn_subln_g,
                    od_w_pw1, od_b_pw1, od_w_dw, od_b_dw, od_ln_g, od_ln_b, od_w_pw2, od_b_pw2,
                    moe_w_router, moe_b_router, moe_w_gu, moe_b_gu, moe_w_down, moe_b_down, final_norm_g)
```
